```python
import math
import jax, jax.numpy as jnp
from jax import lax
import numpy as np

D_MODEL = 4096
BATCH = 4
SEQ = 2048
DEPTH = 1

GRID_W = 64
CTX_LEN = 256
N_MOD = 9
EPS = 1e-6
MACARON_W = 0.5

SSD_EXPAND = 2
SSD_D_INNER = SSD_EXPAND * D_MODEL
SSD_HEAD_DIM = 64
SSD_N_HEADS = SSD_D_INNER // SSD_HEAD_DIM
SSD_N_GROUPS = 8
SSD_D_STATE = 128
SSD_GN = SSD_N_GROUPS * SSD_D_STATE
SSD_CONV_DIM = SSD_D_INNER + 2 * SSD_GN
SSD_CONV = 4
SSD_CHUNK = 128

RG_WIDTH = (D_MODEL * 4 // 3) // 256 * 256
RG_N_BLOCKS = 16
RG_BLOCK = RG_WIDTH // RG_N_BLOCKS
RG_CONV = 4
RG_C = 8.0

D_FF = 11008

_S1 = SSD_D_INNER
_S2 = _S1 + SSD_CONV_DIM
_S3 = _S2 + 2 * SSD_N_HEADS
_S4 = _S3 + RG_WIDTH
_S5 = _S4 + RG_WIDTH
P_IN = _S5 + 2 * D_MODEL

kernel_name = 'hybrid_ssd_rglru_dit_block'


def rms_norm(x, g):
    xf = x.astype(jnp.float32)
    xf = xf * lax.rsqrt(jnp.mean(xf * xf, axis=-1, keepdims=True) + EPS)
    return xf.astype(x.dtype) * g


def mod_slot(mod, k):
    return mod[:, :, 3 * k], mod[:, :, 3 * k + 1], mod[:, :, 3 * k + 2]


def swiglu(u, w_up, w_down):
    gate, up = jnp.split(u @ w_up, 2, axis=-1)
    return (jax.nn.silu(gate) * up) @ w_down


def ffn_sublayer(h, mod, k, g_pre, g_post, w_up, w_down):
    shift, scale, gate = mod_slot(mod, k)
    u = rms_norm(h, g_pre) * (1.0 + scale) + shift
    return h + MACARON_W * gate * rms_norm(swiglu(u, w_up, w_down), g_post)


def centred_dwconv(x, w, b):
    k = w.shape[0]
    left = k // 2
    y = lax.conv_general_dilated(x, w[:, None, :], window_strides=(1,), padding=[(left, k - 1 - left)],
                                 dimension_numbers=('NWC', 'WIO', 'NWC'), feature_group_count=x.shape[-1])
    return y + b


def flip(t):
    return jnp.flip(t, axis=1)


def to_col_major(t, rows):
    b, s, ch = t.shape
    return t.reshape(b, rows, GRID_W, ch).transpose(0, 2, 1, 3).reshape(b, s, ch)


def to_row_major(t, rows):
    b, s, ch = t.shape
    return t.reshape(b, GRID_W, rows, ch).transpose(0, 2, 1, 3).reshape(b, s, ch)


def segsum(a):
    t = a.shape[-1]
    cs = jnp.cumsum(a, axis=-1)
    ss = cs[..., :, None] - cs[..., None, :]
    mask = jnp.tril(jnp.ones((t, t), dtype=bool))
    return jnp.where(mask, ss, -jnp.inf)


def ssd_scan(x, dt, a_log, bm, cm, h0, compute_y):
    bsz, t, h, p = x.shape
    g, n = bm.shape[2], bm.shape[3]
    e = h // g
    nc, cl = t // SSD_CHUNK, SSD_CHUNK
    da = (dt * -jnp.exp(a_log)).astype(jnp.float32)
    xc = (x * dt[..., None]).reshape(bsz, nc, cl, g, e, p)
    bc = bm.reshape(bsz, nc, cl, g, n)
    cc = cm.reshape(bsz, nc, cl, g, n)
    da = da.reshape(bsz, nc, cl, g, e).transpose(0, 3, 4, 1, 2)
    a_cs = jnp.cumsum(da, axis=-1)
    decay_states = jnp.exp(a_cs[..., -1:] - a_cs)
    states = jnp.einsum('bclgn,bgecl,bclgep->bcgepn', bc, decay_states, xc)
    states = jnp.concatenate([h0.reshape(bsz, 1, g, e, p, n), states], axis=1)
    chunk_decay = jnp.exp(segsum(jnp.pad(a_cs[..., -1], ((0, 0), (0, 0), (0, 0), (1, 0)))))
    new_states = jnp.einsum('bgezc,bcgepn->bzgepn', chunk_decay, states)
    final = new_states[:, -1].reshape(bsz, h, p, n)
    if not compute_y:
        return None, final
    prev_states = new_states[:, :-1]
    lmat = jnp.exp(segsum(da))
    cb = jnp.einsum('bclgn,bcsgn->bgcls', cc, bc)
    y_diag = jnp.einsum('bgcls,bgecls,bcsgep->bclgep', cb, lmat, xc)
    y_off = jnp.einsum('bclgn,bcgepn,bgecl->bclgep', cc, prev_states, jnp.exp(a_cs))
    y = (y_diag + y_off).reshape(bsz, t, h, p).astype(x.dtype)
    return y, final


def bidir_ssd(xs, dt_f, dt_b, bm, cm, a_log, h0_f, h0_b, compute_y):
    y_f, s_f = ssd_scan(xs, dt_f, a_log[0], bm, cm, h0_f, compute_y)
    y_b, s_b = ssd_scan(flip(xs), flip(dt_b), a_log[1], flip(bm), flip(cm), h0_b, compute_y)
    y = y_f + flip(y_b) if compute_y else None
    return y, s_f, s_b


def ssd_features(proj, p):
    z, xbc, dt_raw = proj[..., :_S1], proj[..., _S1:_S2], proj[..., _S2:_S3]
    xbc = jax.nn.silu(centred_dwconv(xbc, p['ssd_conv_w'], p['ssd_conv_b']))
    bsz, t, _ = xbc.shape
    xs = xbc[..., :SSD_D_INNER].reshape(bsz, t, SSD_N_HEADS, SSD_HEAD_DIM)
    bm = xbc[..., SSD_D_INNER:SSD_D_INNER + SSD_GN].reshape(bsz, t, SSD_N_GROUPS, SSD_D_STATE)
    cm = xbc[..., SSD_D_INNER + SSD_GN:].reshape(bsz, t, SSD_N_GROUPS, SSD_D_STATE)
    dt = jax.nn.softplus(dt_raw + p['ssd_dt_bias'])
    return z, xs, bm, cm, dt[..., :SSD_N_HEADS], dt[..., SSD_N_HEADS:]


def ssd_output(y, xs, z, p):
    bsz, t = y.shape[0], y.shape[1]
    y = (y + p['ssd_d'][:, None] * xs).reshape(bsz, t, SSD_D_INNER) * jax.nn.silu(z)
    y = rms_norm(y.reshape(bsz, t, SSD_N_GROUPS, SSD_D_INNER // SSD_N_GROUPS),
                 p['ssd_norm_g'].reshape(SSD_N_GROUPS, SSD_D_INNER // SSD_N_GROUPS))
    return y.reshape(bsz, t, SSD_D_INNER) @ p['w_ssd_out']


def rglru_coeffs(xr, w_a, b_a, w_x, b_x, lam):
    bsz, t, w = xr.shape
    xf = xr.astype(jnp.float32)
    xb = xf.reshape(bsz, t, RG_N_BLOCKS, RG_BLOCK)
    r = jax.nn.sigmoid(jnp.einsum('btki,kij->btkj', xb, w_a).reshape(bsz, t, w) + b_a)
    i = jax.nn.sigmoid(jnp.einsum('btki,kij->btkj', xb, w_x).reshape(bsz, t, w) + b_x)
    log_a = -RG_C * r * jax.nn.softplus(-lam)
    a = jnp.exp(log_a)
    b = jnp.sqrt(-jnp.expm1(2.0 * log_a)) * (i * xf)
    return a, b


def linear_scan(a, b, h0):
    b = b.at[:, 0].add(a[:, 0] * h0)
    def combine(left, right):
        return left[0] * right[0], right[0] * left[1] + right[1]
    _, h = lax.associative_scan(combine, (a, b), axis=1)
    return h


def bidir_rglru(xr, p, h0_f, h0_b, compute_y):
    a, b = rglru_coeffs(xr, p['rg_w_a'][0], p['rg_b_a'][0], p['rg_w_x'][0], p['rg_b_x'][0], p['rg_lam'][0])
    h_f = linear_scan(a, b, h0_f)
    a, b = rglru_coeffs(flip(xr), p['rg_w_a'][1], p['rg_b_a'][1], p['rg_w_x'][1], p['rg_b_x'][1], p['rg_lam'][1])
    h_b = linear_scan(a, b, h0_b)
    y = (h_f + flip(h_b)).astype(xr.dtype) if compute_y else None
    return y, h_f[:, -1], h_b[:, -1]


def merge_branches(gates_raw, y_ssd, y_rg, w_out):
    g_ssd, g_rg = jnp.split(jax.nn.sigmoid(gates_raw), 2, axis=-1)
    return (g_ssd * y_ssd + g_rg * y_rg) @ w_out


def token_mixer(u_ctx, u_lat, rows, p, with_ctx_out):
    pc = u_ctx @ p['w_in']
    pl = u_lat @ p['w_in']
    bsz = u_lat.shape[0]
    zc, xc, bc, cc, dfc, dbc = ssd_features(pc, p)
    zl, xl, bl, cl, dfl, dbl = ssd_features(pl, p)
    s0 = jnp.zeros((bsz, SSD_N_HEADS, SSD_HEAD_DIM, SSD_D_STATE), jnp.float32)
    yc, sf, sb = bidir_ssd(xc, dfc, dbc, bc, cc, p['ssd_a_log'], s0, s0, with_ctx_out)
    yl, _, _ = bidir_ssd(xl, dfl, dbl, bl, cl, p['ssd_a_log'], sf, sb, True)
    ssd_lat = ssd_output(yl, xl, zl, p)
    xr_c = centred_dwconv(pc[..., _S4:_S5], p['rg_conv_w'], p['rg_conv_b'])
    xr_l = centred_dwconv(to_col_major(pl[..., _S4:_S5], rows), p['rg_conv_w'], p['rg_conv_b'])
    r0 = jnp.zeros((bsz, RG_WIDTH), jnp.float32)
    hc, rf, rb = bidir_rglru(xr_c, p, r0, r0, with_ctx_out)
    hl, _, _ = bidir_rglru(xr_l, p, rf, rb, True)
    rg_lat = (jax.nn.gelu(pl[..., _S3:_S4]) * to_row_major(hl, rows)) @ p['w_rg_out']
    out_lat = merge_branches(pl[..., _S5:], ssd_lat, rg_lat, p['w_out'])
    if not with_ctx_out:
        return None, out_lat
    ssd_ctx = ssd_output(yc, xc, zc, p)
    rg_ctx = (jax.nn.gelu(pc[..., _S3:_S4]) * hc) @ p['w_rg_out']
    out_ctx = merge_branches(pc[..., _S5:], ssd_ctx, rg_ctx, p['w_out'])
    return out_ctx, out_lat


def setup_inputs(seed: int = 0) -> dict:
    key = jax.random.key(seed)
    ks = jax.random.split(key, 32)
    L = DEPTH
    H = SSD_N_HEADS
    def nrm(k, shape, scale):
        return jax.random.normal(k, shape, jnp.float32) * scale
    dt0 = jnp.exp(jax.random.uniform(ks[12], (L, 2 * H), jnp.float32, math.log(1e-3), math.log(1e-1)))
    u = jax.random.uniform(ks[22], (L, 2, RG_WIDTH), jnp.float32, 0.9, 0.999)
    s = u ** (1.0 / RG_C)
    return {
        'x': nrm(ks[0], (BATCH, SEQ, D_MODEL), 1.0),
        'c': nrm(ks[1], (BATCH, D_MODEL), 1.0),
        'ctx': nrm(ks[2], (BATCH, CTX_LEN, D_MODEL), 1.0),
        'c_ctx': nrm(ks[3], (D_MODEL,), 1.0),
        'w_ada': nrm(ks[4], (L, D_MODEL, N_MOD * D_MODEL), 0.5 * D_MODEL ** -0.5),
        'b_ada': nrm(ks[5], (L, N_MOD * D_MODEL), 0.02),
        'norm_g': 1.0 + nrm(ks[6], (L, 6, D_MODEL), 0.02),
        'ffn_w_up': nrm(ks[7], (L, 2, D_MODEL, 2 * D_FF), D_MODEL ** -0.5),
        'ffn_w_down': nrm(ks[8], (L, 2, D_FF, D_MODEL), D_FF ** -0.5),
        'w_in': nrm(ks[9], (L, D_MODEL, P_IN), D_MODEL ** -0.5),
        'ssd_conv_w': nrm(ks[10], (L, SSD_CONV, SSD_CONV_DIM), SSD_CONV ** -0.5),
        'ssd_conv_b': nrm(ks[11], (L, SSD_CONV_DIM), 0.02),
        'ssd_dt_bias': dt0 + jnp.log(-jnp.expm1(-dt0)),
        'ssd_a_log': jnp.log(jax.random.uniform(ks[13], (L, 2, H), jnp.float32, 1.0, 16.0)),
        'ssd_d': 1.0 + nrm(ks[14], (L, H), 0.02),
        'ssd_norm_g': 1.0 + nrm(ks[15], (L, SSD_D_INNER), 0.02),
        'w_ssd_out': nrm(ks[16], (L, SSD_D_INNER, D_MODEL), SSD_D_INNER ** -0.5),
        'rg_conv_w': nrm(ks[17], (L, RG_CONV, RG_WIDTH), RG_CONV ** -0.5),
        'rg_conv_b': nrm(ks[18], (L, RG_WIDTH), 0.02),
        'rg_w_a': nrm(ks[19], (L, 2, RG_N_BLOCKS, RG_BLOCK, RG_BLOCK), RG_BLOCK ** -0.5),
        'rg_b_a': nrm(ks[20], (L, 2, RG_WIDTH), 0.02),
        'rg_w_x': nrm(ks[21], (L, 2, RG_N_BLOCKS, RG_BLOCK, RG_BLOCK), RG_BLOCK ** -0.5),
        'rg_b_x': nrm(ks[23], (L, 2, RG_WIDTH), 0.02),
        'rg_lam': jnp.log(s) - jnp.log1p(-s),
        'w_rg_out': nrm(ks[24], (L, RG_WIDTH, D_MODEL), RG_WIDTH ** -0.5),
        'w_out': nrm(ks[25], (L, D_MODEL, D_MODEL), D_MODEL ** -0.5),
    }


def reference(x, c, ctx, c_ctx, w_ada, b_ada, norm_g, ffn_w_up, ffn_w_down, w_in,
              ssd_conv_w, ssd_conv_b, ssd_dt_bias, ssd_a_log, ssd_d, ssd_norm_g, w_ssd_out,
              rg_conv_w, rg_conv_b, rg_w_a, rg_b_a, rg_w_x, rg_b_x, rg_lam, w_rg_out, w_out):
    rows = x.shape[1] // GRID_W
    bsz = x.shape[0]
    h_lat, h_ctx = x, ctx
    sc, scc = jax.nn.silu(c), jax.nn.silu(c_ctx)
    for l in range(DEPTH):
        last = l == DEPTH - 1
        mod_lat = (sc @ w_ada[l] + b_ada[l]).reshape(bsz, 1, N_MOD, D_MODEL)
        mod_ctx = (scc @ w_ada[l] + b_ada[l]).reshape(1, 1, N_MOD, D_MODEL)
        g = norm_g[l]
        p = {
            'w_in': w_in[l], 'ssd_conv_w': ssd_conv_w[l], 'ssd_conv_b': ssd_conv_b[l],
            'ssd_dt_bias': ssd_dt_bias[l], 'ssd_a_log': ssd_a_log[l], 'ssd_d': ssd_d[l],
            'ssd_norm_g': ssd_norm_g[l], 'w_ssd_out': w_ssd_out[l],
            'rg_conv_w': rg_conv_w[l], 'rg_conv_b': rg_conv_b[l], 'rg_w_a': rg_w_a[l], 'rg_b_a': rg_b_a[l],
            'rg_w_x': rg_w_x[l], 'rg_b_x': rg_b_x[l], 'rg_lam': rg_lam[l], 'w_rg_out': w_rg_out[l],
            'w_out': w_out[l],
        }
        h_lat = ffn_sublayer(h_lat, mod_lat, 0, g[0], g[1], ffn_w_up[l, 0], ffn_w_down[l, 0])
        h_ctx = ffn_sublayer(h_ctx, mod_ctx, 0, g[0], g[1], ffn_w_up[l, 0], ffn_w_down[l, 0])
        sh_l, scl_l, gt_l = mod_slot(mod_lat, 1)
        sh_c, scl_c, gt_c = mod_slot(mod_ctx, 1)
        u_lat = rms_norm(h_lat, g[2]) * (1.0 + scl_l) + sh_l
        u_ctx = rms_norm(h_ctx, g[2]) * (1.0 + scl_c) + sh_c
        m_ctx, m_lat = token_mixer(u_ctx, u_lat, rows, p, not last)
        h_lat = h_lat + gt_l * rms_norm(m_lat, g[3])
        h_lat = ffn_sublayer(h_lat, mod_lat, 2, g[4], g[5], ffn_w_up[l, 1], ffn_w_down[l, 1])
        if not last:
            h_ctx = h_ctx + gt_c * rms_norm(m_ctx, g[3])
            h_ctx = ffn_sublayer(h_ctx, mod_ctx, 2, g[4], g[5], ffn_w_up[l, 1], ffn_w_down[l, 1])
    return h_lat
```

```python
import functools
import math

import jax
import jax.numpy as jnp
from jax import lax
from jax.experimental import pallas as pl
from jax.experimental.pallas import tpu as pltpu

F32 = jnp.float32
BF16 = jnp.bfloat16

D_MODEL = 4096
BATCH = 4
SEQ = 2048
GRID_W = 64
GRID_H = SEQ // GRID_W
CTX_LEN = 256
N_MOD = 9
EPS = 1e-6
MACARON_W = 0.5

N_LAT = BATCH * SEQ
N_CTX = BATCH * CTX_LEN
N_TOK = N_LAT + N_CTX

SSD_D_INNER = 2 * D_MODEL
SSD_HEAD_DIM = 64
SSD_N_HEADS = SSD_D_INNER // SSD_HEAD_DIM
SSD_N_GROUPS = 8
SSD_HEADS_PER_GROUP = SSD_N_HEADS // SSD_N_GROUPS
SSD_GROUP_W = SSD_D_INNER // SSD_N_GROUPS
SSD_D_STATE = 128
SSD_GN = SSD_N_GROUPS * SSD_D_STATE
SSD_CONV_DIM = SSD_D_INNER + 2 * SSD_GN
SSD_CHUNK = 128

RG_WIDTH = 5376
RG_N_BLOCKS = 16
RG_BLOCK = RG_WIDTH // RG_N_BLOCKS
RG_BLOCK_P = 384
RG_WIDTH_P = RG_N_BLOCKS * RG_BLOCK_P
RG_C = 8.0
RG_SEG = 8
RG_CB = 4 * RG_BLOCK_P

D_FF = 11008

_S1 = SSD_D_INNER
_S2 = _S1 + SSD_CONV_DIM
_S3 = _S2 + 2 * SSD_N_HEADS
_S4 = _S3 + RG_WIDTH
_S5 = _S4 + RG_WIDTH

P_Z = 0
P_XBC = P_Z + SSD_D_INNER
P_GG = P_XBC + SSD_CONV_DIM
P_XR = P_GG + RG_WIDTH_P
P_MG = P_XR + RG_WIDTH_P
P_TOT = P_MG + 2 * D_MODEL
DT_W = SSD_N_GROUPS * 128

VMEM_CAP = 56 * 1024 * 1024


def _params(sem, vmem_bytes):
    return pltpu.CompilerParams(dimension_semantics=sem,
                                vmem_limit_bytes=int(min(VMEM_CAP, max(32 << 20, vmem_bytes))))


def _softplus(x):
    return jnp.maximum(x, 0.0) + jnp.log1p(jnp.exp(-jnp.abs(x)))


def _sigmoid(x):
    return jax.nn.sigmoid(x)


def _neg_expm1(x):
    t = jnp.tanh(0.5 * x)
    return 2.0 * t / (t - 1.0)


ADA_BN = 512


def _ada_kernel(c_ref, w_ref, b_ref, o_ref):
    c = c_ref[...]
    sc = (c * _sigmoid(c)).astype(BF16)
    o_ref[...] = jnp.dot(sc, w_ref[...].astype(BF16), preferred_element_type=F32) + b_ref[...]


def _ada(c8, w_ada, b_ada):
    n = w_ada.shape[1]
    return pl.pallas_call(
        _ada_kernel,
        grid=(n // ADA_BN,),
        in_specs=[pl.BlockSpec((8, D_MODEL), lambda j: (0, 0)),
                  pl.BlockSpec((D_MODEL, ADA_BN), lambda j: (0, j)),
                  pl.BlockSpec((1, ADA_BN), lambda j: (0, j))],
        out_specs=pl.BlockSpec((8, ADA_BN), lambda j: (0, j)),
        out_shape=jax.ShapeDtypeStruct((8, n), F32),
        compiler_params=_params(("parallel",), 2 * D_MODEL * ADA_BN * 4 + D_MODEL * ADA_BN * 4 + (4 << 20)),
        name="ada_mod",
    )(c8, w_ada, b_ada)


ROW_BM = 256


def _mod_row(i):
    return jnp.where(i < N_LAT // ROW_BM, i // (SEQ // ROW_BM), BATCH)


def _mod_spec(slot):
    return pl.BlockSpec((None, 1, D_MODEL), lambda i: (_mod_row(i), 0, slot))


def _vec_spec():
    return pl.BlockSpec((1, D_MODEL), lambda i: (0, 0))


def _row_spec():
    return pl.BlockSpec((ROW_BM, D_MODEL), lambda i: (i, 0))


def _rms(x):
    return x * lax.rsqrt(jnp.mean(x * x, axis=-1, keepdims=True) + EPS)


def _normmod_kernel(h_ref, g_ref, sc_ref, sh_ref, u_ref):
    u = _rms(h_ref[...]) * g_ref[...]
    u_ref[...] = (u * (1.0 + sc_ref[...]) + sh_ref[...]).astype(BF16)


def _normmod(h, g, mod3, slot):
    rows = h.shape[0]
    return pl.pallas_call(
        _normmod_kernel,
        grid=(rows // ROW_BM,),
        in_specs=[_row_spec(), _vec_spec(), _mod_spec(3 * slot + 1), _mod_spec(3 * slot)],
        out_specs=_row_spec(),
        out_shape=jax.ShapeDtypeStruct((rows, D_MODEL), BF16),
        compiler_params=_params(("parallel",), 32 << 20),
        name="normmod",
    )(h, g.reshape(1, D_MODEL), mod3, mod3)


def _res_kernel(h_ref, y_ref, gpost_ref, gate_ref, *rest, coef, with_next):
    h2 = h_ref[...] + (coef * gate_ref[...]) * (_rms(y_ref[...]) * gpost_ref[...])
    if with_next:
        gn_ref, sc_ref, sh_ref, ho_ref, u_ref = rest
        ho_ref[...] = h2
        u = _rms(h2) * gn_ref[...]
        u_ref[...] = (u * (1.0 + sc_ref[...]) + sh_ref[...]).astype(BF16)
    else:
        (ho_ref,) = rest
        ho_ref[...] = h2


def _residual(h, y, g_post, mod3, slot, coef, nxt=None):
    rows = y.shape[0]
    in_specs = [_row_spec(), _row_spec(), _vec_spec(), _mod_spec(3 * slot + 2)]
    args = [h, y, g_post.reshape(1, D_MODEL), mod3]
    out_specs = [_row_spec()]
    out_shape = [jax.ShapeDtypeStruct((rows, D_MODEL), F32)]
    if nxt is not None:
        g_next, slot_next = nxt
        in_specs += [_vec_spec(), _mod_spec(3 * slot_next + 1), _mod_spec(3 * slot_next)]
        args += [g_next.reshape(1, D_MODEL), mod3, mod3]
        out_specs.append(_row_spec())
        out_shape.append(jax.ShapeDtypeStruct((rows, D_MODEL), BF16))
    out = pl.pallas_call(
        functools.partial(_res_kernel, coef=coef, with_next=nxt is not None),
        grid=(rows // ROW_BM,),
        in_specs=in_specs, out_specs=out_specs, out_shape=out_shape,
        compiler_params=_params(("parallel",), 40 << 20),
        name="residual_norm",
    )(*args)
    return out if nxt is not None else out[0]


def _up_kernel(x_ref, wg_ref, wu_ref, o_ref):
    x = x_ref[...]
    g = jnp.dot(x, wg_ref[...], preferred_element_type=F32)
    u = jnp.dot(x, wu_ref[...], preferred_element_type=F32)
    o_ref[...] = (g * _sigmoid(g) * u).astype(BF16)


def _ffn_up(u, wg, wu, bm=1024, bn=512):
    rows, k = u.shape
    n = wg.shape[1]
    vmem = 2 * bm * k * 2 + 4 * k * bn * 2 + 2 * bm * bn * 2 + 3 * bm * bn * 4 + (4 << 20)
    return pl.pallas_call(
        _up_kernel,
        grid=(rows // bm, pl.cdiv(n, bn)),
        in_specs=[pl.BlockSpec((bm, k), lambda i, j: (i, 0)),
                  pl.BlockSpec((k, bn), lambda i, j: (0, j)),
                  pl.BlockSpec((k, bn), lambda i, j: (0, j))],
        out_specs=pl.BlockSpec((bm, bn), lambda i, j: (i, j)),
        out_shape=jax.ShapeDtypeStruct((rows, n), BF16),
        compiler_params=_params(("parallel", "parallel"), vmem),
        name="ffn_up_swiglu",
    )(u, wg, wu)


def _mm_kernel(x_ref, w_ref, *rest, mode):
    acc = jnp.dot(x_ref[...], w_ref[...], preferred_element_type=F32)
    if mode == "plain":
        (o_ref,) = rest
    elif mode == "gate":
        g_ref, o_ref = rest
        acc = _sigmoid(g_ref[...].astype(F32)) * acc
    else:
        g_ref, p_ref, o_ref = rest
        acc = p_ref[...] + _sigmoid(g_ref[...].astype(F32)) * acc
    o_ref[...] = acc.astype(o_ref.dtype)


def _matmul(x, w, *, rows, bm, bn, out_dtype, gate=None, prev=None, single_buffer_x=False, name="matmul"):
    k = x.shape[1]
    n = w.shape[1]
    mode = "plain" if gate is None else ("gate" if prev is None else "gate_add")
    x_kwargs = dict(pipeline_mode=pl.Buffered(1)) if single_buffer_x else {}
    in_specs = [pl.BlockSpec((bm, k), lambda i, j: (i, 0), **x_kwargs),
                pl.BlockSpec((k, bn), lambda i, j: (0, j))]
    args = [x, w]
    if gate is not None:
        g_arr, g_blk0 = gate
        in_specs.append(pl.BlockSpec((bm, bn), lambda i, j: (i, g_blk0 + j)))
        args.append(g_arr)
    if prev is not None:
        in_specs.append(pl.BlockSpec((bm, bn), lambda i, j: (i, j)))
        args.append(prev)
    xbuf = 1 if single_buffer_x else 2
    vmem = xbuf * bm * k * 2 + 2 * k * bn * 2 + 6 * bm * bn * 4 + (4 << 20)
    return pl.pallas_call(
        functools.partial(_mm_kernel, mode=mode),
        grid=(rows // bm, n // bn),
        in_specs=in_specs,
        out_specs=pl.BlockSpec((bm, bn), lambda i, j: (i, j)),
        out_shape=jax.ShapeDtypeStruct((rows, n), out_dtype),
        compiler_params=_params(("parallel", "parallel"), vmem),
        name=name,
    )(*args)


def _ffn(u, wg, wu, wd):
    act = _ffn_up(u, wg, wu)
    return _matmul(act, wd, rows=act.shape[0], bm=1024, bn=256, out_dtype=F32,
                   single_buffer_x=True, name="ffn_down")


def _conv_seq_kernel(x_ref, w_ref, b_ref, o_ref, *, silu):
    x = x_ref[...].astype(F32)
    t = x.shape[0]
    row = lax.broadcasted_iota(jnp.int32, x.shape, 0)
    w = w_ref[...]
    acc = x * w[2:3] + b_ref[...]
    acc += jnp.where(row >= 2, pltpu.roll(x, 2, axis=0), 0.0) * w[0:1]
    acc += jnp.where(row >= 1, pltpu.roll(x, 1, axis=0), 0.0) * w[1:2]
    acc += jnp.where(row < t - 1, pltpu.roll(x, t - 1, axis=0), 0.0) * w[3:4]
    if silu:
        acc = acc * _sigmoid(acc)
    o_ref[...] = acc.astype(o_ref.dtype)


CONV_CB = 256


def _conv_seq(src, w, b, *, seq, first_row_block, n_seq, col0, width, silu, out_dtype, out_rows, out_row_block0,
              alias=None):
    cb0 = col0 // CONV_CB
    in_specs = [pl.BlockSpec((seq, CONV_CB), lambda s, j: (first_row_block + s, cb0 + j)),
                pl.BlockSpec((4, CONV_CB), lambda s, j: (0, j)),
                pl.BlockSpec((1, CONV_CB), lambda s, j: (0, j))]
    args = [src, w, b]
    io_alias = {}
    if alias is not None:
        in_specs.append(pl.BlockSpec(memory_space=pl.ANY))
        args.append(alias)
        io_alias = {3: 0}
    kern = functools.partial(_conv_seq_kernel, silu=silu)
    if alias is not None:
        kern = lambda x, w_, b_, a_, o: _conv_seq_kernel(x, w_, b_, o, silu=silu)
    return pl.pallas_call(
        kern,
        grid=(n_seq, width // CONV_CB),
        in_specs=in_specs,
        out_specs=pl.BlockSpec((seq, CONV_CB), lambda s, j: (out_row_block0 + s, j)),
        out_shape=jax.ShapeDtypeStruct((out_rows, width), out_dtype),
        input_output_aliases=io_alias,
        compiler_params=_params(("parallel", "parallel"), 48 << 20),
        name="conv_seq",
    )(*args)


def _split3(v):
    hi = v.astype(BF16)
    r1 = v - hi.astype(F32)
    mid = r1.astype(BF16)
    lo = (r1 - mid.astype(F32)).astype(BF16)
    return hi, mid, lo


def _dot_exact_rhs(lhs_bf16, v, terms=3):
    out = None
    for piece in _split3(v)[:terms]:
        d = jnp.dot(lhs_bf16, piece, preferred_element_type=F32)
        out = d if out is None else out + d
    return out


def _dot_exact_lhs(v, rhs_bf16, terms=3):
    out = None
    for piece in _split3(v)[:terms]:
        d = jnp.dot(piece, rhs_bf16, preferred_element_type=F32)
        out = d if out is None else out + d
    return out


def _ssd_kernel(x_ref, b_ref, c_ref, dt_ref, bias_ref, alog_ref, ebc_ref, exp_ref, y_ref, state_ref, csb_ref, *, rev):
    L = SSD_CHUNK
    E = SSD_HEADS_PER_GROUP
    P = SSD_HEAD_DIM
    off = E if rev else 0
    s = pl.program_id(2)

    @pl.when(s == 0)
    def _():
        state_ref[...] = jnp.zeros_like(state_ref)

    ii = lax.broadcasted_iota(jnp.int32, (L, L), 0)
    jj = lax.broadcasted_iota(jnp.int32, (L, L), 1)
    tri = (jj >= ii) if rev else (jj <= ii)
    tri_b = jnp.where(tri, 1.0, 0.0).astype(BF16)

    dt = _softplus(dt_ref[...] + bias_ref[...])
    da = dt * (-jnp.exp(alog_ref[...]))
    cs = _dot_exact_rhs(tri_b, da)
    tot = cs[0:1] if rev else cs[L - 1:L]
    to_end = dt * jnp.exp(tot - cs)
    from_start = jnp.exp(cs)

    x = x_ref[...]
    bm = b_ref[...]
    cm = c_ref[...]
    expand = exp_ref[...]
    w_end = _dot_exact_lhs(to_end, expand, terms=2)
    w_start = _dot_exact_lhs(from_start, expand, terms=2)
    state = state_ref[...]

    @pl.when(s >= 2)
    def _():
        csb_ref[...] = _dot_exact_lhs(cs, ebc_ref[...])
        cs_t = cs.T
        dt_t = dt.T
        cb = lax.dot_general(cm, bm, (((1,), (1,)), ((), ())), preferred_element_type=F32)
        y_off = jnp.dot(cm, state.astype(BF16), preferred_element_type=F32) * w_start
        lane = lax.broadcasted_iota(jnp.int32, (L, 2 * P), 1)
        for pair in range(E // 2):
            halves = []
            for e in (2 * pair, 2 * pair + 1):
                seg = csb_ref[:, e * L:(e + 1) * L] - cs_t[off + e:off + e + 1, :]
                lmat = jnp.exp(jnp.where(tri, seg, -1e30))
                m = (cb * lmat * dt_t[off + e:off + e + 1, :]).astype(BF16)
                halves.append(jnp.dot(m, x[:, pair * 2 * P:(pair + 1) * 2 * P], preferred_element_type=F32))
            y_pair = jnp.where(lane < P, halves[0], halves[1]) + y_off[:, pair * 2 * P:(pair + 1) * 2 * P]
            y_ref[:, pair * 2 * P:(pair + 1) * 2 * P] = y_pair.astype(y_ref.dtype)

    x_end = (x.astype(F32) * w_end).astype(BF16)
    upd = lax.dot_general(bm, x_end, (((0,), (0,)), ((), ())), preferred_element_type=F32)
    chunk_decay = w_start[0:1] if rev else w_start[L - 1:L]
    state_ref[...] = state * chunk_decay + upd


def _ssd_row_block(b, s, rev):
    n_lat = SEQ // SSD_CHUNK
    if rev:
        ctx = N_LAT // SSD_CHUNK + 2 * b + (1 - s)
        lat = n_lat * b + (n_lat + 1 - s)
    else:
        ctx = N_LAT // SSD_CHUNK + 2 * b + s
        lat = n_lat * b + (s - 2)
    return jnp.where(s < 2, ctx, lat)


def _ssd_scan(xconv, dtp, bias_p, alog_p, ebc, expand, rev):
    n_lat = SEQ // SSD_CHUNK
    rb = functools.partial(_ssd_row_block, rev=rev)

    def out_block(b, s):
        return _ssd_row_block(b, jnp.maximum(s, 2), rev)

    gx = SSD_D_INNER // SSD_D_STATE
    return pl.pallas_call(
        functools.partial(_ssd_kernel, rev=rev),
        grid=(BATCH, SSD_N_GROUPS, n_lat + 2),
        in_specs=[pl.BlockSpec((SSD_CHUNK, SSD_GROUP_W), lambda b, g, s: (rb(b, s), g)),
                  pl.BlockSpec((SSD_CHUNK, SSD_D_STATE), lambda b, g, s: (rb(b, s), gx + g)),
                  pl.BlockSpec((SSD_CHUNK, SSD_D_STATE), lambda b, g, s: (rb(b, s), gx + SSD_N_GROUPS + g)),
                  pl.BlockSpec((SSD_CHUNK, 128), lambda b, g, s: (rb(b, s), g)),
                  pl.BlockSpec((1, 128), lambda b, g, s: (0, g)),
                  pl.BlockSpec((1, 128), lambda b, g, s: (0, g)),
                  pl.BlockSpec(ebc.shape, lambda b, g, s: (0, 0)),
                  pl.BlockSpec(expand.shape, lambda b, g, s: (0, 0))],
        out_specs=pl.BlockSpec((SSD_CHUNK, SSD_GROUP_W), lambda b, g, s: (out_block(b, s), g)),
        out_shape=jax.ShapeDtypeStruct((N_LAT, SSD_D_INNER), BF16),
        scratch_shapes=[pltpu.VMEM((SSD_D_STATE, SSD_GROUP_W), F32),
                        pltpu.VMEM((SSD_CHUNK, SSD_HEADS_PER_GROUP * SSD_CHUNK), F32)],
        compiler_params=_params(("parallel", "parallel", "arbitrary"), 32 << 20),
        name="ssd_scan_bwd" if rev else "ssd_scan_fwd",
    )(xconv, xconv, xconv, dtp, bias_p, alog_p, ebc, expand)


def _ssd_gate_kernel(yf_ref, yb_ref, xs_ref, z_ref, d_ref, g_ref, o_ref):
    z = z_ref[...].astype(F32)
    y = yf_ref[...].astype(F32) + yb_ref[...].astype(F32) + d_ref[...] * xs_ref[...].astype(F32)
    y = y * (z * _sigmoid(z))
    o_ref[...] = (_rms(y) * g_ref[...]).astype(BF16)


def _ssd_gate(yf, yb, xconv, proj, d_exp, norm_g, bm=512):
    blk = lambda: pl.BlockSpec((bm, SSD_GROUP_W), lambda i, g: (i, g))
    vec = lambda: pl.BlockSpec((1, SSD_GROUP_W), lambda i, g: (0, g))
    return pl.pallas_call(
        _ssd_gate_kernel,
        grid=(N_LAT // bm, SSD_N_GROUPS),
        in_specs=[blk(), blk(), blk(), blk(), vec(), vec()],
        out_specs=blk(),
        out_shape=jax.ShapeDtypeStruct((N_LAT, SSD_D_INNER), BF16),
        compiler_params=_params(("parallel", "parallel"), 32 << 20),
        name="ssd_gate_norm",
    )(yf, yb, xconv, proj, d_exp, norm_g)


def _rg_conv_lat_kernel(x_ref, w_ref, b_ref, o_ref):
    w = w_ref[...]
    bias = b_ref[...]
    col = lax.broadcasted_iota(jnp.int32, (GRID_W, x_ref.shape[2]), 0)

    def row(r):
        return x_ref[r].astype(F32)

    def prev_col(v):
        return jnp.where(col >= 1, pltpu.roll(v, 1, axis=0), 0.0)

    def next_col(v):
        return jnp.where(col < GRID_W - 1, pltpu.roll(v, GRID_W - 1, axis=0), 0.0)

    for r in range(GRID_H):
        m2 = row(r - 2) if r >= 2 else prev_col(row(r - 2 + GRID_H))
        m1 = row(r - 1) if r >= 1 else prev_col(row(GRID_H - 1))
        p1 = row(r + 1) if r < GRID_H - 1 else next_col(row(0))
        o_ref[r] = m2 * w[0:1] + m1 * w[1:2] + row(r) * w[2:3] + p1 * w[3:4] + bias


def _rg_conv_lat(proj3, w, b):
    cb0 = P_XR // CONV_CB
    return pl.pallas_call(
        _rg_conv_lat_kernel,
        grid=(BATCH, RG_WIDTH_P // CONV_CB),
        in_specs=[pl.BlockSpec((GRID_H, GRID_W, CONV_CB), lambda bb, j: (bb, 0, cb0 + j)),
                  pl.BlockSpec((4, CONV_CB), lambda bb, j: (0, j)),
                  pl.BlockSpec((1, CONV_CB), lambda bb, j: (0, j))],
        out_specs=pl.BlockSpec((GRID_H, GRID_W, CONV_CB), lambda bb, j: (bb, 0, j)),
        out_shape=jax.ShapeDtypeStruct((BATCH * GRID_H, GRID_W, RG_WIDTH_P), F32),
        compiler_params=_params(("parallel", "parallel"), 32 << 20),
        name="rg_conv_lat",
    )(proj3, w, b)


def _rg_scan_kernel(xl_ref, xc_ref, wa_ref, wx_ref, ba_ref, bx_ref, lam_ref, h_ref, carry_ref, a_ref, b_ref, *, rev):
    s = pl.program_id(2)

    @pl.when(s == 0)
    def _():
        carry_ref[...] = jnp.zeros_like(carry_ref)

    x3 = jnp.where(s == 0, xc_ref[...], xl_ref[...])
    x2 = x3.reshape(GRID_H * RG_SEG, RG_CB)
    xb = x2.astype(BF16)
    neg_c_sp = -RG_C * _softplus(-lam_ref[...])
    for k in range(RG_CB // RG_BLOCK_P):
        sl = slice(k * RG_BLOCK_P, (k + 1) * RG_BLOCK_P)
        xk = xb[:, sl]
        r = _sigmoid(jnp.dot(xk, wa_ref[k], preferred_element_type=F32) + ba_ref[:, sl])
        i = _sigmoid(jnp.dot(xk, wx_ref[k], preferred_element_type=F32) + bx_ref[:, sl])
        log_a = neg_c_sp[:, sl] * r
        a_ref[:, :, sl] = jnp.exp(log_a).reshape(GRID_H, RG_SEG, RG_BLOCK_P)
        b_ref[:, :, sl] = (jnp.sqrt(_neg_expm1(2.0 * log_a)) * (i * x2[:, sl])).reshape(GRID_H, RG_SEG, RG_BLOCK_P)

    order = range(GRID_H - 1, -1, -1) if rev else range(GRID_H)
    p = None
    for r in order:
        a = a_ref[r]
        if p is None:
            p, hloc = a, b_ref[r]
        else:
            p, hloc = a * p, a * hloc + b_ref[r]
        a_ref[r] = p
        b_ref[r] = hloc
    seg_row = lax.broadcasted_iota(jnp.int32, (RG_SEG, RG_CB), 0)
    state = carry_ref[...]
    enter = jnp.zeros((RG_SEG, RG_CB), F32)
    for c in (range(RG_SEG - 1, -1, -1) if rev else range(RG_SEG)):
        enter = jnp.where(seg_row == c, state, enter)
        state = hloc[c:c + 1] + p[c:c + 1] * state
    carry_ref[...] = state

    @pl.when(s >= 1)
    def _():
        for r in range(GRID_H):
            h_ref[r] = b_ref[r] + a_ref[r] * enter


def _rg_scan(xl3, xc3, wa, wx, ba, bx, lam, rev):
    n_segblk = GRID_W // RG_SEG

    def seg_block(s):
        return jnp.clip((n_segblk - s) if rev else (s - 1), 0, n_segblk - 1)

    blk = (GRID_H, RG_SEG, RG_CB)
    nk = RG_CB // RG_BLOCK_P
    wspec = lambda: pl.BlockSpec((nk, RG_BLOCK_P, RG_BLOCK_P), lambda cb, b, s: (cb, 0, 0))
    vspec = lambda: pl.BlockSpec((1, RG_CB), lambda cb, b, s: (0, cb))
    return pl.pallas_call(
        functools.partial(_rg_scan_kernel, rev=rev),
        grid=(RG_WIDTH_P // RG_CB, BATCH, n_segblk + 1),
        in_specs=[pl.BlockSpec(blk, lambda cb, b, s: (b, seg_block(s), cb)),
                  pl.BlockSpec(blk, lambda cb, b, s: (b, 0, cb)),
                  wspec(), wspec(), vspec(), vspec(), vspec()],
        out_specs=pl.BlockSpec(blk, lambda cb, b, s: (b, seg_block(s), cb)),
        out_shape=jax.ShapeDtypeStruct((BATCH * GRID_H, GRID_W, RG_WIDTH_P), F32),
        scratch_shapes=[pltpu.VMEM((1, RG_CB), F32), pltpu.VMEM(blk, F32), pltpu.VMEM(blk, F32)],
        compiler_params=_params(("parallel", "parallel", "arbitrary"), 40 << 20),
        name="rg_scan_bwd" if rev else "rg_scan_fwd",
    )(xl3, xc3, wa, wx, ba, bx, lam)


def _rg_gate_kernel(g_ref, hf_ref, hb_ref, o_ref):
    g = g_ref[...].astype(F32)
    o_ref[...] = (jax.nn.gelu(g) * (hf_ref[...] + hb_ref[...])).astype(BF16)


def _rg_gate(proj, hf, hb, bm=512, bn=1536):
    c0 = P_GG // bn
    return pl.pallas_call(
        _rg_gate_kernel,
        grid=(N_LAT // bm, RG_WIDTH_P // bn),
        in_specs=[pl.BlockSpec((bm, bn), lambda i, j: (i, c0 + j)),
                  pl.BlockSpec((bm, bn), lambda i, j: (i, j)),
                  pl.BlockSpec((bm, bn), lambda i, j: (i, j))],
        out_specs=pl.BlockSpec((bm, bn), lambda i, j: (i, j)),
        out_shape=jax.ShapeDtypeStruct((N_LAT, RG_WIDTH_P), BF16),
        compiler_params=_params(("parallel", "parallel"), 32 << 20),
        name="rg_gate",
    )(proj, hf, hb)


def _pad_rg(a, axis):
    axis = axis % a.ndim
    shape = a.shape[:axis] + (RG_N_BLOCKS, RG_BLOCK) + a.shape[axis + 1:]
    pad = [(0, 0)] * (a.ndim + 1)
    pad[axis + 1] = (0, RG_BLOCK_P - RG_BLOCK)
    out = jnp.pad(a.reshape(shape), pad)
    return out.reshape(a.shape[:axis] + (RG_WIDTH_P,) + a.shape[axis + 1:])


def _group_lanes(a):
    lead = a.shape[:-1]
    e = SSD_HEADS_PER_GROUP
    t = a.reshape(lead + (2, SSD_N_GROUPS, e))
    t = jnp.moveaxis(t, -3, -2).reshape(lead + (SSD_N_GROUPS, 2 * e))
    t = jnp.pad(t, [(0, 0)] * len(lead) + [(0, 0), (0, 128 - 2 * e)])
    return t.reshape(lead + (DT_W,))


def _one_hot_maps(rev):
    e = SSD_HEADS_PER_GROUP
    k = jnp.arange(128)[:, None]
    off = e if rev else 0
    head = jnp.where((k >= off) & (k < off + e), k - off, -1)
    ebc = (head == (jnp.arange(e * SSD_CHUNK)[None, :] // SSD_CHUNK)).astype(BF16)
    expand = (head == (jnp.arange(e * SSD_HEAD_DIM)[None, :] // SSD_HEAD_DIM)).astype(BF16)
    return ebc, expand


def kernel(x, c, ctx, c_ctx, w_ada, b_ada, norm_g, ffn_w_up, ffn_w_down, w_in, ssd_conv_w, ssd_conv_b, ssd_dt_bias,
           ssd_a_log, ssd_d, ssd_norm_g, w_ssd_out, rg_conv_w, rg_conv_b, rg_w_a, rg_b_a, rg_w_x, rg_b_x, rg_lam,
           w_rg_out, w_out):
    l = 0
    g = norm_g[l]

    wg = [ffn_w_up[l, k, :, :D_FF].astype(BF16) for k in range(2)]
    wu = [ffn_w_up[l, k, :, D_FF:].astype(BF16) for k in range(2)]
    wd = [ffn_w_down[l, k].astype(BF16) for k in range(2)]
    wi = w_in[l]
    w_proj = jnp.concatenate([wi[:, :_S2], _pad_rg(wi[:, _S3:_S4], 1), _pad_rg(wi[:, _S4:_S5], 1), wi[:, _S5:]],
                             axis=1).astype(BF16)
    w_dt = _group_lanes(wi[:, _S2:_S3]).astype(BF16)
    dt_bias_p = _group_lanes(ssd_dt_bias[l]).reshape(1, DT_W)
    alog_p = _group_lanes(ssd_a_log[l].reshape(2 * SSD_N_HEADS)).reshape(1, DT_W)
    d_exp = jnp.repeat(ssd_d[l], SSD_HEAD_DIM).reshape(1, SSD_D_INNER)
    ssd_ng = ssd_norm_g[l].reshape(1, SSD_D_INNER)
    ssd_cw, ssd_cb = ssd_conv_w[l], ssd_conv_b[l].reshape(1, SSD_CONV_DIM)
    rg_cw, rg_cb = _pad_rg(rg_conv_w[l], 1), _pad_rg(rg_conv_b[l], 0).reshape(1, RG_WIDTH_P)
    pad_w = lambda w: jnp.pad(w, ((0, 0), (0, RG_BLOCK_P - RG_BLOCK), (0, RG_BLOCK_P - RG_BLOCK))).astype(BF16)
    rg_wa = [pad_w(rg_w_a[l, d]) for d in range(2)]
    rg_wx = [pad_w(rg_w_x[l, d]) for d in range(2)]
    rg_ba = [_pad_rg(rg_b_a[l, d], 0).reshape(1, RG_WIDTH_P) for d in range(2)]
    rg_bx = [_pad_rg(rg_b_x[l, d], 0).reshape(1, RG_WIDTH_P) for d in range(2)]
    rg_lm = [_pad_rg(rg_lam[l, d], 0).reshape(1, RG_WIDTH_P) for d in range(2)]
    w_rg_o = _pad_rg(w_rg_out[l], 0).astype(BF16)
    w_ssd_o = w_ssd_out[l].astype(BF16)
    w_o = w_out[l].astype(BF16)

    c8 = jnp.concatenate([c, c_ctx[None], jnp.zeros((8 - BATCH - 1, D_MODEL), F32)], axis=0)
    mod3 = _ada(c8, w_ada[l], b_ada[l].reshape(1, -1)).reshape(8, 1, N_MOD * D_MODEL)

    h0 = jnp.concatenate([x.reshape(N_LAT, D_MODEL), ctx.reshape(N_CTX, D_MODEL)], axis=0)

    u1 = _normmod(h0, g[0], mod3, 0)
    y1 = _ffn(u1, wg[0], wu[0], wd[0])
    h1, u2 = _residual(h0, y1, g[1], mod3, 0, MACARON_W, nxt=(g[2], 1))

    proj = _matmul(u2, w_proj, rows=N_TOK, bm=1024, bn=1024, out_dtype=BF16, name="in_proj")
    dtp = _matmul(u2, w_dt, rows=N_TOK, bm=1024, bn=DT_W, out_dtype=F32, name="dt_proj")

    xconv = _conv_seq(proj, ssd_cw, ssd_cb, seq=SEQ, first_row_block=0, n_seq=BATCH, col0=P_XBC, width=SSD_CONV_DIM,
                      silu=True, out_dtype=BF16, out_rows=N_TOK, out_row_block0=0)
    xconv = _conv_seq(proj, ssd_cw, ssd_cb, seq=CTX_LEN, first_row_block=N_LAT // CTX_LEN, n_seq=BATCH, col0=P_XBC,
                      width=SSD_CONV_DIM, silu=True, out_dtype=BF16, out_rows=N_TOK,
                      out_row_block0=N_LAT // CTX_LEN, alias=xconv)
    y_f = _ssd_scan(xconv, dtp, dt_bias_p, alog_p, *_one_hot_maps(False), rev=False)
    y_b = _ssd_scan(xconv, dtp, dt_bias_p, alog_p, *_one_hot_maps(True), rev=True)
    y_n = _ssd_gate(y_f, y_b, xconv, proj, d_exp, ssd_ng)

    xr_lat = _rg_conv_lat(proj.reshape(N_TOK // GRID_W, GRID_W, P_TOT), rg_cw, rg_cb)
    xr_ctx = _conv_seq(proj, rg_cw, rg_cb, seq=CTX_LEN, first_row_block=N_LAT // CTX_LEN, n_seq=BATCH, col0=P_XR,
                       width=RG_WIDTH_P, silu=False, out_dtype=F32, out_rows=N_CTX, out_row_block0=0)
    xr_ctx = xr_ctx.reshape(BATCH, RG_SEG, GRID_H, RG_WIDTH_P).transpose(0, 2, 1, 3)
    xr_ctx = xr_ctx.reshape(BATCH * GRID_H, RG_SEG, RG_WIDTH_P)
    h_f = _rg_scan(xr_lat, xr_ctx, rg_wa[0], rg_wx[0], rg_ba[0], rg_bx[0], rg_lm[0], rev=False)
    h_b = _rg_scan(xr_lat, xr_ctx, rg_wa[1], rg_wx[1], rg_ba[1], rg_bx[1], rg_lm[1], rev=True)
    r_in = _rg_gate(proj, h_f.reshape(N_LAT, RG_WIDTH_P), h_b.reshape(N_LAT, RG_WIDTH_P))

    m1 = _matmul(y_n, w_ssd_o, rows=N_LAT, bm=512, bn=512, out_dtype=F32, gate=(proj, P_MG // 512),
                 name="ssd_out_proj")
    m2 = _matmul(r_in, w_rg_o, rows=N_LAT, bm=1024, bn=512, out_dtype=BF16,
                 gate=(proj, (P_MG + D_MODEL) // 512), prev=m1, name="rg_out_proj")
    m3 = _matmul(m2, w_o, rows=N_LAT, bm=1024, bn=512, out_dtype=F32, name="out_proj")
    h2, u3 = _residual(h1, m3, g[3], mod3, 1, 1.0, nxt=(g[4], 2))

    y3 = _ffn(u3, wg[1], wu[1], wd[1])
    out = _residual(h2, y3, g[5], mod3, 2, MACARON_W)
    return out.reshape(BATCH, SEQ, D_MODEL)
```

```python
import functools
import math

import jax
import jax.numpy as jnp
from jax import lax
from jax.experimental import pallas as pl
from jax.experimental.pallas import tpu as pltpu

F32 = jnp.float32
BF16 = jnp.bfloat16

D_MODEL = 4096
BATCH = 4
SEQ = 2048
GRID_W = 64
GRID_H = SEQ // GRID_W
CTX_LEN = 256
N_MOD = 9
EPS = 1e-6
MACARON_W = 0.5

N_LAT = BATCH * SEQ
N_CTX = BATCH * CTX_LEN
N_TOK = N_LAT + N_CTX

SSD_D_INNER = 2 * D_MODEL
SSD_HEAD_DIM = 64
SSD_N_HEADS = SSD_D_INNER // SSD_HEAD_DIM
SSD_N_GROUPS = 8
SSD_HEADS_PER_GROUP = SSD_N_HEADS // SSD_N_GROUPS
SSD_GROUP_W = SSD_D_INNER // SSD_N_GROUPS
SSD_D_STATE = 128
SSD_GN = SSD_N_GROUPS * SSD_D_STATE
SSD_CONV_DIM = SSD_D_INNER + 2 * SSD_GN
SSD_CHUNK = 128
SSD_GROUPS_PER_STEP = 2

RG_WIDTH = 5376
RG_N_BLOCKS = 16
RG_BLOCK = RG_WIDTH // RG_N_BLOCKS
RG_BLOCK_P = 384
RG_WIDTH_P = RG_N_BLOCKS * RG_BLOCK_P
RG_C = 8.0
RG_SEG = 8
RG_CB = 4 * RG_BLOCK_P

D_FF = 11008

_S1 = SSD_D_INNER
_S2 = _S1 + SSD_CONV_DIM
_S3 = _S2 + 2 * SSD_N_HEADS
_S4 = _S3 + RG_WIDTH
_S5 = _S4 + RG_WIDTH

DT_W = SSD_N_GROUPS * 128

VMEM_CAP = 56 * 1024 * 1024


def _params(sem, vmem_bytes):
    return pltpu.CompilerParams(dimension_semantics=sem,
                                vmem_limit_bytes=int(min(VMEM_CAP, max(32 << 20, vmem_bytes))))


def _softplus(x):
    return jnp.maximum(x, 0.0) + jnp.log1p(jnp.exp(-jnp.abs(x)))


def _sigmoid(x):
    return 0.5 * jnp.tanh(0.5 * x) + 0.5


def _neg_expm1(x):
    t = jnp.tanh(0.5 * x)
    return 2.0 * t / (t - 1.0)


ADA_BN = 512


def _ada_kernel(c_ref, w_ref, b_ref, o_ref):
    c = c_ref[...]
    sc = (c * _sigmoid(c)).astype(BF16)
    o_ref[...] = jnp.dot(sc, w_ref[...].astype(BF16), preferred_element_type=F32) + b_ref[...]


def _ada(c8, w_ada, b_ada):
    n = w_ada.shape[1]
    return pl.pallas_call(
        _ada_kernel,
        grid=(n // ADA_BN,),
        in_specs=[pl.BlockSpec((8, D_MODEL), lambda j: (0, 0)),
                  pl.BlockSpec((D_MODEL, ADA_BN), lambda j: (0, j)),
                  pl.BlockSpec((1, ADA_BN), lambda j: (0, j))],
        out_specs=pl.BlockSpec((8, ADA_BN), lambda j: (0, j)),
        out_shape=jax.ShapeDtypeStruct((8, n), F32),
        compiler_params=_params(("parallel",), 2 * D_MODEL * ADA_BN * 4 + D_MODEL * ADA_BN * 4 + (4 << 20)),
        name="ada_mod",
    )(c8, w_ada, b_ada)


ROW_BM = 256


def _mod_row(i):
    return jnp.where(i < N_LAT // ROW_BM, i // (SEQ // ROW_BM), BATCH)


def _mod_spec(slot):
    return pl.BlockSpec((None, 1, D_MODEL), lambda i: (_mod_row(i), 0, slot))


def _vec_spec():
    return pl.BlockSpec((1, D_MODEL), lambda i: (0, 0))


def _row_spec():
    return pl.BlockSpec((ROW_BM, D_MODEL), lambda i: (i, 0))


N_LAT_BLK = N_LAT // ROW_BM


def _token_specs(h):
    if not isinstance(h, tuple):
        return [_row_spec()], [h]
    return [pl.BlockSpec((ROW_BM, D_MODEL), lambda i: (jnp.minimum(i, N_LAT_BLK - 1), 0)),
            pl.BlockSpec((ROW_BM, D_MODEL), lambda i: (jnp.maximum(i - N_LAT_BLK, 0), 0))], list(h)


def _load_tokens(refs):
    if len(refs) == 1:
        return refs[0][...]
    return jnp.where(pl.program_id(0) < N_LAT_BLK, refs[0][...], refs[1][...])


def _rms(x):
    return x * lax.rsqrt(jnp.mean(x * x, axis=-1, keepdims=True) + EPS)


def _normmod_kernel(*refs, n_h):
    g_ref, sc_ref, sh_ref, u_ref = refs[n_h:]
    u = _rms(_load_tokens(refs[:n_h])) * g_ref[...]
    u_ref[...] = (u * (1.0 + sc_ref[...]) + sh_ref[...]).astype(BF16)


def _normmod(h, g, mod3, slot, rows):
    h_specs, h_args = _token_specs(h)
    return pl.pallas_call(
        functools.partial(_normmod_kernel, n_h=len(h_args)),
        grid=(rows // ROW_BM,),
        in_specs=h_specs + [_vec_spec(), _mod_spec(3 * slot + 1), _mod_spec(3 * slot)],
        out_specs=_row_spec(),
        out_shape=jax.ShapeDtypeStruct((rows, D_MODEL), BF16),
        compiler_params=_params(("parallel",), 40 << 20),
        name="normmod",
    )(*h_args, g.reshape(1, D_MODEL), mod3, mod3)


def _res_kernel(*refs, n_h, coef, with_next):
    y_ref, gpost_ref, gate_ref = refs[n_h:n_h + 3]
    rest = refs[n_h + 3:]
    h2 = _load_tokens(refs[:n_h]) + (coef * gate_ref[...]) * (_rms(y_ref[...]) * gpost_ref[...])
    if with_next:
        gn_ref, sc_ref, sh_ref, ho_ref, u_ref = rest
        ho_ref[...] = h2
        u = _rms(h2) * gn_ref[...]
        u_ref[...] = (u * (1.0 + sc_ref[...]) + sh_ref[...]).astype(BF16)
    else:
        (ho_ref,) = rest
        ho_ref[...] = h2


def _residual(h, y, g_post, mod3, slot, coef, nxt=None):
    rows = y.shape[0]
    h_specs, h_args = _token_specs(h)
    in_specs = h_specs + [_row_spec(), _vec_spec(), _mod_spec(3 * slot + 2)]
    args = h_args + [y, g_post.reshape(1, D_MODEL), mod3]
    out_specs = [_row_spec()]
    out_shape = [jax.ShapeDtypeStruct((rows, D_MODEL), F32)]
    if nxt is not None:
        g_next, slot_next = nxt
        in_specs += [_vec_spec(), _mod_spec(3 * slot_next + 1), _mod_spec(3 * slot_next)]
        args += [g_next.reshape(1, D_MODEL), mod3, mod3]
        out_specs.append(_row_spec())
        out_shape.append(jax.ShapeDtypeStruct((rows, D_MODEL), BF16))
    out = pl.pallas_call(
        functools.partial(_res_kernel, n_h=len(h_args), coef=coef, with_next=nxt is not None),
        grid=(rows // ROW_BM,),
        in_specs=in_specs, out_specs=out_specs, out_shape=out_shape,
        compiler_params=_params(("parallel",), 48 << 20),
        name="residual_norm",
    )(*args)
    return out if nxt is not None else out[0]


def _up_kernel(x_ref, wg_ref, wu_ref, o_ref):
    x = x_ref[...]
    g = jnp.dot(x, wg_ref[...], preferred_element_type=F32)
    u = jnp.dot(x, wu_ref[...], preferred_element_type=F32)
    o_ref[...] = (g * _sigmoid(g) * u).astype(BF16)


def _ffn_up(u, wg, wu, bm=1024, bn=512):
    rows, k = u.shape
    n = wg.shape[1]
    vmem = 2 * bm * k * 2 + 4 * k * bn * 2 + 2 * bm * bn * 2 + 3 * bm * bn * 4 + (4 << 20)
    return pl.pallas_call(
        _up_kernel,
        grid=(rows // bm, pl.cdiv(n, bn)),
        in_specs=[pl.BlockSpec((bm, k), lambda i, j: (i, 0)),
                  pl.BlockSpec((k, bn), lambda i, j: (0, j)),
                  pl.BlockSpec((k, bn), lambda i, j: (0, j))],
        out_specs=pl.BlockSpec((bm, bn), lambda i, j: (i, j)),
        out_shape=jax.ShapeDtypeStruct((rows, n), BF16),
        compiler_params=_params(("parallel", "parallel"), vmem),
        name="ffn_up_swiglu",
    )(u, wg, wu)


def _mm_kernel(x_ref, w_ref, *rest, mode):
    acc = jnp.dot(x_ref[...], w_ref[...], preferred_element_type=F32)
    if mode == "plain":
        (o_ref,) = rest
    elif mode == "softplus_bias":
        b_ref, o_ref = rest
        acc = _softplus(acc + b_ref[...])
    elif mode == "gate":
        g_ref, o_ref = rest
        acc = _sigmoid(g_ref[...].astype(F32)) * acc
    else:
        g_ref, p_ref, o_ref = rest
        acc = p_ref[...] + _sigmoid(g_ref[...].astype(F32)) * acc
    o_ref[...] = acc.astype(o_ref.dtype)


def _matmul(x, w, *, rows, bm, bn, out_dtype, gate=None, prev=None, softplus_bias=None, single_buffer_x=False,
            name="matmul"):
    k = x.shape[1]
    n = w.shape[1]
    mode = "plain" if gate is None else ("gate" if prev is None else "gate_add")
    x_kwargs = dict(pipeline_mode=pl.Buffered(1)) if single_buffer_x else {}
    in_specs = [pl.BlockSpec((bm, k), lambda i, j: (i, 0), **x_kwargs),
                pl.BlockSpec((k, bn), lambda i, j: (0, j))]
    args = [x, w]
    if softplus_bias is not None:
        mode = "softplus_bias"
        in_specs.append(pl.BlockSpec((1, bn), lambda i, j: (0, j)))
        args.append(softplus_bias)
    if gate is not None:
        g_arr, g_blk0 = gate
        in_specs.append(pl.BlockSpec((bm, bn), lambda i, j: (i, g_blk0 + j)))
        args.append(g_arr)
    if prev is not None:
        in_specs.append(pl.BlockSpec((bm, bn), lambda i, j: (i, j)))
        args.append(prev)
    xbuf = 1 if single_buffer_x else 2
    vmem = xbuf * bm * k * 2 + 2 * k * bn * 2 + 6 * bm * bn * 4 + (4 << 20)
    return pl.pallas_call(
        functools.partial(_mm_kernel, mode=mode),
        grid=(rows // bm, n // bn),
        in_specs=in_specs,
        out_specs=pl.BlockSpec((bm, bn), lambda i, j: (i, j)),
        out_shape=jax.ShapeDtypeStruct((rows, n), out_dtype),
        compiler_params=_params(("parallel", "parallel"), vmem),
        name=name,
    )(*args)


def _ffn(u, wg, wu, wd):
    act = _ffn_up(u, wg, wu)
    return _matmul(act, wd, rows=act.shape[0], bm=1024, bn=512, out_dtype=F32,
                   single_buffer_x=True, name="ffn_down")


def _conv_seq_kernel(x_ref, w_ref, b_ref, o_ref, *, silu):
    x = x_ref[...].astype(F32)
    t = x.shape[0]
    row = lax.broadcasted_iota(jnp.int32, x.shape, 0)
    w = w_ref[...]
    acc = x * w[2:3] + b_ref[...]
    acc += jnp.where(row >= 2, pltpu.roll(x, 2, axis=0), 0.0) * w[0:1]
    acc += jnp.where(row >= 1, pltpu.roll(x, 1, axis=0), 0.0) * w[1:2]
    acc += jnp.where(row < t - 1, pltpu.roll(x, t - 1, axis=0), 0.0) * w[3:4]
    if silu:
        acc = acc * _sigmoid(acc)
    o_ref[...] = acc.astype(o_ref.dtype)


CONV_CB = 256


def _conv_seq(src, w, b, *, seq, first_row_block, n_seq, col0, width, silu, out_dtype, out_rows, out_row_block0,
              alias=None):
    cb0 = col0 // CONV_CB
    in_specs = [pl.BlockSpec((seq, CONV_CB), lambda s, j: (first_row_block + s, cb0 + j)),
                pl.BlockSpec((4, CONV_CB), lambda s, j: (0, j)),
                pl.BlockSpec((1, CONV_CB), lambda s, j: (0, j))]
    args = [src, w, b]
    io_alias = {}
    if alias is not None:
        in_specs.append(pl.BlockSpec(memory_space=pl.ANY))
        args.append(alias)
        io_alias = {3: 0}
    kern = functools.partial(_conv_seq_kernel, silu=silu)
    if alias is not None:
        kern = lambda x, w_, b_, a_, o: _conv_seq_kernel(x, w_, b_, o, silu=silu)
    return pl.pallas_call(
        kern,
        grid=(n_seq, width // CONV_CB),
        in_specs=in_specs,
        out_specs=pl.BlockSpec((seq, CONV_CB), lambda s, j: (out_row_block0 + s, j)),
        out_shape=jax.ShapeDtypeStruct((out_rows, width), out_dtype),
        input_output_aliases=io_alias,
        compiler_params=_params(("parallel", "parallel"), 48 << 20),
        name="conv_seq",
    )(*args)


def _split3(v):
    hi = v.astype(BF16)
    r1 = v - hi.astype(F32)
    mid = r1.astype(BF16)
    lo = (r1 - mid.astype(F32)).astype(BF16)
    return hi, mid, lo


def _dot_exact_rhs(lhs_bf16, v, terms=3):
    out = None
    for piece in _split3(v)[:terms]:
        d = jnp.dot(lhs_bf16, piece, preferred_element_type=F32)
        out = d if out is None else out + d
    return out


def _ssd_kernel(x_ref, b_ref, c_ref, dt_ref, alog_ref, y_ref, state_ref, *, rev):
    L = SSD_CHUNK
    E = SSD_HEADS_PER_GROUP
    P = SSD_HEAD_DIM
    GW = SSD_GROUP_W
    off = E if rev else 0
    s = pl.program_id(2)

    @pl.when(s == 0)
    def _():
        state_ref[...] = jnp.zeros_like(state_ref)

    ii = lax.broadcasted_iota(jnp.int32, (L, L), 0)
    jj = lax.broadcasted_iota(jnp.int32, (L, L), 1)
    tri = (jj >= ii) if rev else (jj <= ii)
    tri_b = jnp.where(tri, 1.0, 0.0).astype(BF16)
    first_half = lax.broadcasted_iota(jnp.int32, (L, 2 * P), 1) < P

    groups = []
    for gi in range(SSD_GROUPS_PER_STEP):
        tile = slice(gi * 128, (gi + 1) * 128)
        dt = dt_ref[:, tile]
        da = dt * (-jnp.exp(alog_ref[:, tile]))
        cs = _dot_exact_rhs(tri_b, da)
        tot = cs[0:1] if rev else cs[L - 1:L]
        log_end = jnp.log(dt) + (tot - cs)
        groups.append((dt, cs, tot, log_end, b_ref[:, tile], c_ref[:, tile]))

    @pl.when(s >= 2)
    def _():
        for gi, (dt, cs, tot, log_end, bm, cm) in enumerate(groups):
            cs_t = cs.T
            dt_t = dt.T
            cb = lax.dot_general(cm, bm, (((1,), (1,)), ((), ())), preferred_element_type=F32)
            for pair in range(E // 2):
                sl = slice(gi * GW + pair * 2 * P, gi * GW + (pair + 1) * 2 * P)
                xp = x_ref[:, sl]
                halves, cols = [], []
                for e in (off + 2 * pair, off + 2 * pair + 1):
                    col = jnp.broadcast_to(cs[:, e:e + 1], (L, L))
                    cols.append(col)
                    lmat = jnp.exp(jnp.where(tri, col - cs_t[e:e + 1, :], -1e30))
                    m = (cb * lmat * dt_t[e:e + 1, :]).astype(BF16)
                    halves.append(jnp.dot(m, xp, preferred_element_type=F32))
                from_start = jnp.exp(jnp.where(first_half, cols[0], cols[1]))
                y_off = jnp.dot(cm, state_ref[:, sl].astype(BF16), preferred_element_type=F32) * from_start
                y_ref[:, sl] = (jnp.where(first_half, halves[0], halves[1]) + y_off).astype(y_ref.dtype)

    for gi, (dt, cs, tot, log_end, bm, cm) in enumerate(groups):
        for pair in range(E // 2):
            sl = slice(gi * GW + pair * 2 * P, gi * GW + (pair + 1) * 2 * P)
            e0 = off + 2 * pair
            w_end = jnp.exp(jnp.where(first_half, jnp.broadcast_to(log_end[:, e0:e0 + 1], (L, 2 * P)),
                                      jnp.broadcast_to(log_end[:, e0 + 1:e0 + 2], (L, 2 * P))))
            x_end = (x_ref[:, sl].astype(F32) * w_end).astype(BF16)
            upd = lax.dot_general(bm, x_end, (((0,), (0,)), ((), ())), preferred_element_type=F32)
            tot_pair = jnp.where(first_half[0:1], jnp.broadcast_to(tot[:, e0:e0 + 1], (1, 2 * P)),
                                 jnp.broadcast_to(tot[:, e0 + 1:e0 + 2], (1, 2 * P)))
            state_ref[:, sl] = state_ref[:, sl] * jnp.exp(tot_pair) + upd


def _ssd_row_block(b, s, rev):
    n_lat = SEQ // SSD_CHUNK
    if rev:
        ctx = N_LAT // SSD_CHUNK + 2 * b + (1 - s)
        lat = n_lat * b + (n_lat + 1 - s)
    else:
        ctx = N_LAT // SSD_CHUNK + 2 * b + s
        lat = n_lat * b + (s - 2)
    return jnp.where(s < 2, ctx, lat)


def _ssd_scan(xconv, dt, alog_p, rev):
    n_lat = SEQ // SSD_CHUNK
    gps = SSD_GROUPS_PER_STEP
    rb = functools.partial(_ssd_row_block, rev=rev)

    def out_block(b, s):
        return _ssd_row_block(b, jnp.maximum(s, 2), rev)

    bw = gps * SSD_D_STATE
    b0 = SSD_D_INNER // bw
    c0 = (SSD_D_INNER + SSD_GN) // bw
    return pl.pallas_call(
        functools.partial(_ssd_kernel, rev=rev),
        grid=(BATCH, SSD_N_GROUPS // gps, n_lat + 2),
        in_specs=[pl.BlockSpec((SSD_CHUNK, gps * SSD_GROUP_W), lambda b, g, s: (rb(b, s), g)),
                  pl.BlockSpec((SSD_CHUNK, bw), lambda b, g, s: (rb(b, s), b0 + g)),
                  pl.BlockSpec((SSD_CHUNK, bw), lambda b, g, s: (rb(b, s), c0 + g)),
                  pl.BlockSpec((SSD_CHUNK, gps * 128), lambda b, g, s: (rb(b, s), g)),
                  pl.BlockSpec((1, gps * 128), lambda b, g, s: (0, g))],
        out_specs=pl.BlockSpec((SSD_CHUNK, gps * SSD_GROUP_W), lambda b, g, s: (out_block(b, s), g)),
        out_shape=jax.ShapeDtypeStruct((N_LAT, SSD_D_INNER), BF16),
        scratch_shapes=[pltpu.VMEM((SSD_D_STATE, gps * SSD_GROUP_W), F32)],
        compiler_params=_params(("parallel", "parallel", "arbitrary"), 32 << 20),
        name="ssd_scan_bwd" if rev else "ssd_scan_fwd",
    )(xconv, xconv, xconv, dt, alog_p)


def _ssd_gate_kernel(yf_ref, yb_ref, xs_ref, z_ref, d_ref, g_ref, o_ref):
    z = z_ref[...].astype(F32)
    y = yf_ref[...].astype(F32) + yb_ref[...].astype(F32) + d_ref[...] * xs_ref[...].astype(F32)
    y = y * (z * _sigmoid(z))
    o_ref[...] = (_rms(y) * g_ref[...]).astype(BF16)


def _ssd_gate(yf, yb, xconv, proj, d_exp, norm_g, bm=512):
    blk = lambda: pl.BlockSpec((bm, SSD_GROUP_W), lambda i, g: (i, g))
    vec = lambda: pl.BlockSpec((1, SSD_GROUP_W), lambda i, g: (0, g))
    return pl.pallas_call(
        _ssd_gate_kernel,
        grid=(N_LAT // bm, SSD_N_GROUPS),
        in_specs=[blk(), blk(), blk(), blk(), vec(), vec()],
        out_specs=blk(),
        out_shape=jax.ShapeDtypeStruct((N_LAT, SSD_D_INNER), BF16),
        compiler_params=_params(("parallel", "parallel"), 32 << 20),
        name="ssd_gate_norm",
    )(yf, yb, xconv, proj, d_exp, norm_g)


def _rg_conv_lat_kernel(x_ref, w_ref, b_ref, o_ref):
    w = w_ref[...]
    bias = b_ref[...]
    col = lax.broadcasted_iota(jnp.int32, (GRID_W, x_ref.shape[2]), 0)

    def row(r):
        return x_ref[r].astype(F32)

    def prev_col(v):
        return jnp.where(col >= 1, pltpu.roll(v, 1, axis=0), 0.0)

    def next_col(v):
        return jnp.where(col < GRID_W - 1, pltpu.roll(v, GRID_W - 1, axis=0), 0.0)

    for r in range(GRID_H):
        m2 = row(r - 2) if r >= 2 else prev_col(row(r - 2 + GRID_H))
        m1 = row(r - 1) if r >= 1 else prev_col(row(GRID_H - 1))
        p1 = row(r + 1) if r < GRID_H - 1 else next_col(row(0))
        o_ref[r] = m2 * w[0:1] + m1 * w[1:2] + row(r) * w[2:3] + p1 * w[3:4] + bias


def _rg_conv_lat(xr3, w, b):
    return pl.pallas_call(
        _rg_conv_lat_kernel,
        grid=(BATCH, RG_WIDTH_P // CONV_CB),
        in_specs=[pl.BlockSpec((GRID_H, GRID_W, CONV_CB), lambda bb, j: (bb, 0, j)),
                  pl.BlockSpec((4, CONV_CB), lambda bb, j: (0, j)),
                  pl.BlockSpec((1, CONV_CB), lambda bb, j: (0, j))],
        out_specs=pl.BlockSpec((GRID_H, GRID_W, CONV_CB), lambda bb, j: (bb, 0, j)),
        out_shape=jax.ShapeDtypeStruct((BATCH * GRID_H, GRID_W, RG_WIDTH_P), F32),
        compiler_params=_params(("parallel", "parallel"), 32 << 20),
        name="rg_conv_lat",
    )(xr3, w, b)


def _rg_scan_kernel(xl_ref, xc_ref, wa_ref, wx_ref, ba_ref, bx_ref, lam_ref, h_ref, carry_ref, a_ref, b_ref, *, rev):
    s = pl.program_id(2)

    @pl.when(s == 0)
    def _():
        carry_ref[...] = jnp.zeros_like(carry_ref)

    x3 = jnp.where(s == 0, xc_ref[...], xl_ref[...])
    x2 = x3.reshape(GRID_H * RG_SEG, RG_CB)
    xb = x2.astype(BF16)
    neg_c_sp = -RG_C * _softplus(-lam_ref[...])
    for k in range(RG_CB // RG_BLOCK_P):
        sl = slice(k * RG_BLOCK_P, (k + 1) * RG_BLOCK_P)
        xk = xb[:, sl]
        r = _sigmoid(jnp.dot(xk, wa_ref[k], preferred_element_type=F32) + ba_ref[:, sl])
        i = _sigmoid(jnp.dot(xk, wx_ref[k], preferred_element_type=F32) + bx_ref[:, sl])
        log_a = neg_c_sp[:, sl] * r
        a_ref[:, :, sl] = jnp.exp(log_a).reshape(GRID_H, RG_SEG, RG_BLOCK_P)
        b_ref[:, :, sl] = (jnp.sqrt(_neg_expm1(2.0 * log_a)) * (i * x2[:, sl])).reshape(GRID_H, RG_SEG, RG_BLOCK_P)

    order = range(GRID_H - 1, -1, -1) if rev else range(GRID_H)
    p = None
    for r in order:
        a = a_ref[r]
        if p is None:
            p, hloc = a, b_ref[r]
        else:
            p, hloc = a * p, a * hloc + b_ref[r]
        a_ref[r] = p
        b_ref[r] = hloc
    seg_row = lax.broadcasted_iota(jnp.int32, (RG_SEG, RG_CB), 0)
    state = carry_ref[...]
    enter = jnp.zeros((RG_SEG, RG_CB), F32)
    for c in (range(RG_SEG - 1, -1, -1) if rev else range(RG_SEG)):
        enter = jnp.where(seg_row == c, state, enter)
        state = hloc[c:c + 1] + p[c:c + 1] * state
    carry_ref[...] = state

    @pl.when(s >= 1)
    def _():
        for r in range(GRID_H):
            h_ref[r] = b_ref[r] + a_ref[r] * enter


def _rg_scan(xl3, xc3, wa, wx, ba, bx, lam, rev):
    n_segblk = GRID_W // RG_SEG

    def seg_block(s):
        return jnp.clip((n_segblk - s) if rev else (s - 1), 0, n_segblk - 1)

    blk = (GRID_H, RG_SEG, RG_CB)
    nk = RG_CB // RG_BLOCK_P
    wspec = lambda: pl.BlockSpec((nk, RG_BLOCK_P, RG_BLOCK_P), lambda cb, b, s: (cb, 0, 0))
    vspec = lambda: pl.BlockSpec((1, RG_CB), lambda cb, b, s: (0, cb))
    return pl.pallas_call(
        functools.partial(_rg_scan_kernel, rev=rev),
        grid=(RG_WIDTH_P // RG_CB, BATCH, n_segblk + 1),
        in_specs=[pl.BlockSpec(blk, lambda cb, b, s: (b, seg_block(s), cb)),
                  pl.BlockSpec(blk, lambda cb, b, s: (b, 0, cb)),
                  wspec(), wspec(), vspec(), vspec(), vspec()],
        out_specs=pl.BlockSpec(blk, lambda cb, b, s: (b, seg_block(s), cb)),
        out_shape=jax.ShapeDtypeStruct((BATCH * GRID_H, GRID_W, RG_WIDTH_P), F32),
        scratch_shapes=[pltpu.VMEM((1, RG_CB), F32), pltpu.VMEM(blk, F32), pltpu.VMEM(blk, F32)],
        compiler_params=_params(("parallel", "parallel", "arbitrary"), 40 << 20),
        name="rg_scan_bwd" if rev else "rg_scan_fwd",
    )(xl3, xc3, wa, wx, ba, bx, lam)


def _rg_gate_kernel(g_ref, hf_ref, hb_ref, o_ref):
    g = g_ref[...].astype(F32)
    o_ref[...] = (jax.nn.gelu(g) * (hf_ref[...] + hb_ref[...])).astype(BF16)


def _rg_gate(gg, hf, hb, bm=512, bn=1536):
    return pl.pallas_call(
        _rg_gate_kernel,
        grid=(N_LAT // bm, RG_WIDTH_P // bn),
        in_specs=[pl.BlockSpec((bm, bn), lambda i, j: (i, j)),
                  pl.BlockSpec((bm, bn), lambda i, j: (i, j)),
                  pl.BlockSpec((bm, bn), lambda i, j: (i, j))],
        out_specs=pl.BlockSpec((bm, bn), lambda i, j: (i, j)),
        out_shape=jax.ShapeDtypeStruct((N_LAT, RG_WIDTH_P), BF16),
        compiler_params=_params(("parallel", "parallel"), 32 << 20),
        name="rg_gate",
    )(gg, hf, hb)


def _pad_rg(a, axis):
    axis = axis % a.ndim
    shape = a.shape[:axis] + (RG_N_BLOCKS, RG_BLOCK) + a.shape[axis + 1:]
    pad = [(0, 0)] * (a.ndim + 1)
    pad[axis + 1] = (0, RG_BLOCK_P - RG_BLOCK)
    out = jnp.pad(a.reshape(shape), pad)
    return out.reshape(a.shape[:axis] + (RG_WIDTH_P,) + a.shape[axis + 1:])


def _group_lanes(a):
    lead = a.shape[:-1]
    e = SSD_HEADS_PER_GROUP
    t = a.reshape(lead + (2, SSD_N_GROUPS, e))
    t = jnp.moveaxis(t, -3, -2).reshape(lead + (SSD_N_GROUPS, 2 * e))
    t = jnp.pad(t, [(0, 0)] * len(lead) + [(0, 0), (0, 128 - 2 * e)])
    return t.reshape(lead + (DT_W,))


def kernel(x, c, ctx, c_ctx, w_ada, b_ada, norm_g, ffn_w_up, ffn_w_down, w_in, ssd_conv_w, ssd_conv_b, ssd_dt_bias,
           ssd_a_log, ssd_d, ssd_norm_g, w_ssd_out, rg_conv_w, rg_conv_b, rg_w_a, rg_b_a, rg_w_x, rg_b_x, rg_lam,
           w_rg_out, w_out):
    l = 0
    g = norm_g[l]

    wg = [ffn_w_up[l, k, :, :D_FF].astype(BF16) for k in range(2)]
    wu = [ffn_w_up[l, k, :, D_FF:].astype(BF16) for k in range(2)]
    wd = [ffn_w_down[l, k].astype(BF16) for k in range(2)]
    wi = w_in[l]
    w_z = wi[:, :_S1].astype(BF16)
    w_xbc = wi[:, _S1:_S2].astype(BF16)
    w_gg = _pad_rg(wi[:, _S3:_S4].astype(BF16), 1)
    w_xr = _pad_rg(wi[:, _S4:_S5].astype(BF16), 1)
    w_mg = wi[:, _S5:].astype(BF16)
    w_dt = _group_lanes(wi[:, _S2:_S3]).astype(BF16)
    dt_bias_p = _group_lanes(ssd_dt_bias[l]).reshape(1, DT_W)
    alog_p = _group_lanes(ssd_a_log[l].reshape(2 * SSD_N_HEADS)).reshape(1, DT_W)
    d_exp = jnp.repeat(ssd_d[l], SSD_HEAD_DIM).reshape(1, SSD_D_INNER)
    ssd_ng = ssd_norm_g[l].reshape(1, SSD_D_INNER)
    ssd_cw, ssd_cb = ssd_conv_w[l], ssd_conv_b[l].reshape(1, SSD_CONV_DIM)
    rg_cw, rg_cb = _pad_rg(rg_conv_w[l], 1), _pad_rg(rg_conv_b[l], 0).reshape(1, RG_WIDTH_P)
    pad_w = lambda w: jnp.pad(w, ((0, 0), (0, RG_BLOCK_P - RG_BLOCK), (0, RG_BLOCK_P - RG_BLOCK))).astype(BF16)
    rg_wa = [pad_w(rg_w_a[l, d]) for d in range(2)]
    rg_wx = [pad_w(rg_w_x[l, d]) for d in range(2)]
    rg_ba = [_pad_rg(rg_b_a[l, d], 0).reshape(1, RG_WIDTH_P) for d in range(2)]
    rg_bx = [_pad_rg(rg_b_x[l, d], 0).reshape(1, RG_WIDTH_P) for d in range(2)]
    rg_lm = [_pad_rg(rg_lam[l, d], 0).reshape(1, RG_WIDTH_P) for d in range(2)]
    w_rg_o = _pad_rg(w_rg_out[l].astype(BF16), 0)
    w_ssd_o = w_ssd_out[l].astype(BF16)
    w_o = w_out[l].astype(BF16)

    c8 = jnp.concatenate([c, c_ctx[None], jnp.zeros((8 - BATCH - 1, D_MODEL), F32)], axis=0)
    mod3 = _ada(c8, w_ada[l], b_ada[l].reshape(1, -1)).reshape(8, 1, N_MOD * D_MODEL)

    h0 = (x.reshape(N_LAT, D_MODEL), ctx.reshape(N_CTX, D_MODEL))

    u1 = _normmod(h0, g[0], mod3, 0, N_TOK)
    y1 = _ffn(u1, wg[0], wu[0], wd[0])
    h1, u2 = _residual(h0, y1, g[1], mod3, 0, MACARON_W, nxt=(g[2], 1))

    proj = functools.partial(_matmul, u2, bm=1024, bn=1024)
    z = proj(w_z, rows=N_LAT, out_dtype=BF16, name="in_proj_z")
    xbc = proj(w_xbc, rows=N_TOK, out_dtype=BF16, name="in_proj_xbc")
    gg = proj(w_gg, rows=N_LAT, out_dtype=BF16, name="in_proj_gelu_gate")
    xr = proj(w_xr, rows=N_TOK, out_dtype=BF16, name="in_proj_rg")
    mg = proj(w_mg, rows=N_LAT, out_dtype=BF16, name="in_proj_merge_gates")
    dt = proj(w_dt, rows=N_TOK, out_dtype=F32, softplus_bias=dt_bias_p, name="in_proj_dt")

    xconv = _conv_seq(xbc, ssd_cw, ssd_cb, seq=SEQ, first_row_block=0, n_seq=BATCH, col0=0, width=SSD_CONV_DIM,
                      silu=True, out_dtype=BF16, out_rows=N_TOK, out_row_block0=0)
    xconv = _conv_seq(xbc, ssd_cw, ssd_cb, seq=CTX_LEN, first_row_block=N_LAT // CTX_LEN, n_seq=BATCH, col0=0,
                      width=SSD_CONV_DIM, silu=True, out_dtype=BF16, out_rows=N_TOK,
                      out_row_block0=N_LAT // CTX_LEN, alias=xconv)
    y_f = _ssd_scan(xconv, dt, alog_p, rev=False)
    y_b = _ssd_scan(xconv, dt, alog_p, rev=True)
    y_n = _ssd_gate(y_f, y_b, xconv, z, d_exp, ssd_ng)

    xr_lat = _rg_conv_lat(xr.reshape(N_TOK // GRID_W, GRID_W, RG_WIDTH_P), rg_cw, rg_cb)
    xr_ctx = _conv_seq(xr, rg_cw, rg_cb, seq=CTX_LEN, first_row_block=N_LAT // CTX_LEN, n_seq=BATCH, col0=0,
                       width=RG_WIDTH_P, silu=False, out_dtype=F32, out_rows=N_CTX, out_row_block0=0)
    xr_ctx = xr_ctx.reshape(BATCH, RG_SEG, GRID_H, RG_WIDTH_P).transpose(0, 2, 1, 3)
    xr_ctx = xr_ctx.reshape(BATCH * GRID_H, RG_SEG, RG_WIDTH_P)
    h_f = _rg_scan(xr_lat, xr_ctx, rg_wa[0], rg_wx[0], rg_ba[0], rg_bx[0], rg_lm[0], rev=False)
    h_b = _rg_scan(xr_lat, xr_ctx, rg_wa[1], rg_wx[1], rg_ba[1], rg_bx[1], rg_lm[1], rev=True)
    r_in = _rg_gate(gg, h_f.reshape(N_LAT, RG_WIDTH_P), h_b.reshape(N_LAT, RG_WIDTH_P))

    m1 = _matmul(y_n, w_ssd_o, rows=N_LAT, bm=1024, bn=512, out_dtype=F32, gate=(mg, 0), single_buffer_x=True,
                 name="ssd_out_proj")
    m2 = _matmul(r_in, w_rg_o, rows=N_LAT, bm=1024, bn=512, out_dtype=BF16, gate=(mg, D_MODEL // 512), prev=m1,
                 name="rg_out_proj")
    m3 = _matmul(m2, w_o, rows=N_LAT, bm=1024, bn=512, out_dtype=F32, name="out_proj")
    h2, u3 = _residual(h1, m3, g[3], mod3, 1, 1.0, nxt=(g[4], 2))

    y3 = _ffn(u3, wg[1], wu[1], wd[1])
    out = _residual(h2, y3, g[5], mod3, 2, MACARON_W)
    return out.reshape(BATCH, SEQ, D_MODEL)
```

```python
import functools
import math

import jax
import jax.numpy as jnp
from jax import lax
from jax.experimental import pallas as pl
from jax.experimental.pallas import tpu as pltpu

F32 = jnp.float32
BF16 = jnp.bfloat16

D_MODEL = 4096
BATCH = 4
SEQ = 2048
GRID_W = 64
GRID_H = SEQ // GRID_W
CTX_LEN = 256
N_MOD = 9
EPS = 1e-6
MACARON_W = 0.5

N_LAT = BATCH * SEQ
N_CTX = BATCH * CTX_LEN
N_TOK = N_LAT + N_CTX

SSD_D_INNER = 2 * D_MODEL
SSD_HEAD_DIM = 64
SSD_N_HEADS = SSD_D_INNER // SSD_HEAD_DIM
SSD_N_GROUPS = 8
SSD_HEADS_PER_GROUP = SSD_N_HEADS // SSD_N_GROUPS
SSD_GROUP_W = SSD_D_INNER // SSD_N_GROUPS
SSD_D_STATE = 128
SSD_GN = SSD_N_GROUPS * SSD_D_STATE
SSD_CONV_DIM = SSD_D_INNER + 2 * SSD_GN
SSD_CHUNK = 128
SSD_GROUPS_PER_STEP = 2

RG_WIDTH = 5376
RG_N_BLOCKS = 16
RG_BLOCK = RG_WIDTH // RG_N_BLOCKS
RG_BLOCK_P = 384
RG_WIDTH_P = RG_N_BLOCKS * RG_BLOCK_P
RG_C = 8.0
RG_SEG = 8
RG_CB = 4 * RG_BLOCK_P

D_FF = 11008

_S1 = SSD_D_INNER
_S2 = _S1 + SSD_CONV_DIM
_S3 = _S2 + 2 * SSD_N_HEADS
_S4 = _S3 + RG_WIDTH
_S5 = _S4 + RG_WIDTH

DT_W = SSD_N_GROUPS * 128

VMEM_CAP = 56 * 1024 * 1024


def _params(sem, vmem_bytes):
    return pltpu.CompilerParams(dimension_semantics=sem,
                                vmem_limit_bytes=int(min(VMEM_CAP, max(32 << 20, vmem_bytes))))


def _softplus(x):
    return jnp.maximum(x, 0.0) + jnp.log1p(jnp.exp(-jnp.abs(x)))


def _sigmoid(x):
    return 0.5 * jnp.tanh(0.5 * x) + 0.5


def _neg_expm1(x):
    t = jnp.tanh(0.5 * x)
    return 2.0 * t / (t - 1.0)


ADA_BN = 512


def _ada_kernel(c_ref, w_ref, b_ref, o_ref):
    c = c_ref[...]
    sc = (c * _sigmoid(c)).astype(BF16)
    o_ref[...] = jnp.dot(sc, w_ref[...].astype(BF16), preferred_element_type=F32) + b_ref[...]


def _ada(c8, w_ada, b_ada):
    n = w_ada.shape[1]
    return pl.pallas_call(
        _ada_kernel,
        grid=(n // ADA_BN,),
        in_specs=[pl.BlockSpec((8, D_MODEL), lambda j: (0, 0)),
                  pl.BlockSpec((D_MODEL, ADA_BN), lambda j: (0, j)),
                  pl.BlockSpec((1, ADA_BN), lambda j: (0, j))],
        out_specs=pl.BlockSpec((8, ADA_BN), lambda j: (0, j)),
        out_shape=jax.ShapeDtypeStruct((8, n), F32),
        compiler_params=_params(("parallel",), 2 * D_MODEL * ADA_BN * 4 + D_MODEL * ADA_BN * 4 + (4 << 20)),
        name="ada_mod",
    )(c8, w_ada, b_ada)


ROW_BM = 256


def _mod_row(i):
    return jnp.where(i < N_LAT // ROW_BM, i // (SEQ // ROW_BM), BATCH)


def _mod_spec(slot):
    return pl.BlockSpec((None, 1, D_MODEL), lambda i: (_mod_row(i), 0, slot))


def _vec_spec():
    return pl.BlockSpec((1, D_MODEL), lambda i: (0, 0))


def _row_spec():
    return pl.BlockSpec((ROW_BM, D_MODEL), lambda i: (i, 0))


N_LAT_BLK = N_LAT // ROW_BM


def _token_specs(h):
    if not isinstance(h, tuple):
        return [_row_spec()], [h]
    return [pl.BlockSpec((ROW_BM, D_MODEL), lambda i: (jnp.minimum(i, N_LAT_BLK - 1), 0)),
            pl.BlockSpec((ROW_BM, D_MODEL), lambda i: (jnp.maximum(i - N_LAT_BLK, 0), 0))], list(h)


def _load_tokens(refs):
    if len(refs) == 1:
        return refs[0][...]
    return jnp.where(pl.program_id(0) < N_LAT_BLK, refs[0][...], refs[1][...])


def _rms(x):
    return x * lax.rsqrt(jnp.mean(x * x, axis=-1, keepdims=True) + EPS)


def _normmod_kernel(*refs, n_h):
    g_ref, sc_ref, sh_ref, u_ref = refs[n_h:]
    u = _rms(_load_tokens(refs[:n_h])) * g_ref[...]
    u_ref[...] = (u * (1.0 + sc_ref[...]) + sh_ref[...]).astype(BF16)


def _normmod(h, g, mod3, slot, rows):
    h_specs, h_args = _token_specs(h)
    return pl.pallas_call(
        functools.partial(_normmod_kernel, n_h=len(h_args)),
        grid=(rows // ROW_BM,),
        in_specs=h_specs + [_vec_spec(), _mod_spec(3 * slot + 1), _mod_spec(3 * slot)],
        out_specs=_row_spec(),
        out_shape=jax.ShapeDtypeStruct((rows, D_MODEL), BF16),
        compiler_params=_params(("parallel",), 40 << 20),
        name="normmod",
    )(*h_args, g.reshape(1, D_MODEL), mod3, mod3)


def _res_kernel(*refs, n_h, coef, with_next):
    y_ref, gpost_ref, gate_ref = refs[n_h:n_h + 3]
    rest = refs[n_h + 3:]
    h2 = _load_tokens(refs[:n_h]) + (coef * gate_ref[...]) * (_rms(y_ref[...]) * gpost_ref[...])
    if with_next:
        gn_ref, sc_ref, sh_ref, ho_ref, u_ref = rest
        ho_ref[...] = h2
        u = _rms(h2) * gn_ref[...]
        u_ref[...] = (u * (1.0 + sc_ref[...]) + sh_ref[...]).astype(BF16)
    else:
        (ho_ref,) = rest
        ho_ref[...] = h2


def _residual(h, y, g_post, mod3, slot, coef, nxt=None):
    rows = y.shape[0]
    h_specs, h_args = _token_specs(h)
    in_specs = h_specs + [_row_spec(), _vec_spec(), _mod_spec(3 * slot + 2)]
    args = h_args + [y, g_post.reshape(1, D_MODEL), mod3]
    out_specs = [_row_spec()]
    out_shape = [jax.ShapeDtypeStruct((rows, D_MODEL), F32)]
    if nxt is not None:
        g_next, slot_next = nxt
        in_specs += [_vec_spec(), _mod_spec(3 * slot_next + 1), _mod_spec(3 * slot_next)]
        args += [g_next.reshape(1, D_MODEL), mod3, mod3]
        out_specs.append(_row_spec())
        out_shape.append(jax.ShapeDtypeStruct((rows, D_MODEL), BF16))
    out = pl.pallas_call(
        functools.partial(_res_kernel, n_h=len(h_args), coef=coef, with_next=nxt is not None),
        grid=(rows // ROW_BM,),
        in_specs=in_specs, out_specs=out_specs, out_shape=out_shape,
        compiler_params=_params(("parallel",), 48 << 20),
        name="residual_norm",
    )(*args)
    return out if nxt is not None else out[0]


def _w_spec(lead, k, bn, col_blk0):
    squeezed = (None,) * len(lead)
    return pl.BlockSpec(squeezed + (k, bn), lambda i, j: tuple(lead) + (0, col_blk0 + j))


def _up_kernel(x_ref, wg_ref, wu_ref, o_ref):
    x = x_ref[...]
    g = jnp.dot(x, wg_ref[...].astype(BF16), preferred_element_type=F32)
    u = jnp.dot(x, wu_ref[...].astype(BF16), preferred_element_type=F32)
    o_ref[...] = (g * _sigmoid(g) * u).astype(BF16)


FFN_UP_BN = 256


def _ffn_up(u, w_up, lead):
    rows, k = u.shape
    bm, bn = rows // 4, FFN_UP_BN
    vmem = bm * k * 2 + 4 * k * bn * 4 + 2 * k * bn * 2 + 2 * bm * bn * 2 + 3 * bm * bn * 4 + (4 << 20)
    return pl.pallas_call(
        _up_kernel,
        grid=(rows // bm, D_FF // bn),
        in_specs=[pl.BlockSpec((bm, k), lambda i, j: (i, 0), pipeline_mode=pl.Buffered(1)),
                  _w_spec(lead, k, bn, 0),
                  _w_spec(lead, k, bn, D_FF // bn)],
        out_specs=pl.BlockSpec((bm, bn), lambda i, j: (i, j)),
        out_shape=jax.ShapeDtypeStruct((rows, D_FF), BF16),
        compiler_params=_params(("parallel", "parallel"), vmem),
        name="ffn_up_swiglu",
    )(u, w_up, w_up)


def _mm_kernel(x_ref, w_ref, *rest, mode):
    acc = jnp.dot(x_ref[...], w_ref[...].astype(BF16), preferred_element_type=F32)
    if mode == "plain":
        (o_ref,) = rest
    elif mode == "softplus_bias":
        b_ref, o_ref = rest
        acc = _softplus(acc + b_ref[...])
    elif mode == "gate":
        g_ref, o_ref = rest
        acc = _sigmoid(g_ref[...].astype(F32)) * acc
    else:
        g_ref, p_ref, o_ref = rest
        acc = p_ref[...] + _sigmoid(g_ref[...].astype(F32)) * acc
    o_ref[...] = acc.astype(o_ref.dtype)


def _matmul(x, w, *, rows, bm, bn, out_dtype, gate=None, prev=None, softplus_bias=None, single_buffer_x=False,
            w_lead=(), w_col0=0, n=None, name="matmul"):
    k = x.shape[1]
    n = w.shape[-1] if n is None else n
    mode = "plain" if gate is None else ("gate" if prev is None else "gate_add")
    x_kwargs = dict(pipeline_mode=pl.Buffered(1)) if single_buffer_x else {}
    in_specs = [pl.BlockSpec((bm, k), lambda i, j: (i, 0), **x_kwargs),
                _w_spec(w_lead, k, bn, w_col0 // bn)]
    args = [x, w]
    if softplus_bias is not None:
        mode = "softplus_bias"
        in_specs.append(pl.BlockSpec((1, bn), lambda i, j: (0, j)))
        args.append(softplus_bias)
    if gate is not None:
        g_arr, g_blk0 = gate
        in_specs.append(pl.BlockSpec((bm, bn), lambda i, j: (i, g_blk0 + j)))
        args.append(g_arr)
    if prev is not None:
        in_specs.append(pl.BlockSpec((bm, bn), lambda i, j: (i, j)))
        args.append(prev)
    xbuf = 1 if single_buffer_x else 2
    w_bytes = 2 * k * bn * w.dtype.itemsize + (k * bn * 2 if w.dtype != BF16 else 0)
    vmem = xbuf * bm * k * 2 + w_bytes + 5 * bm * bn * 4 + (4 << 20)
    return pl.pallas_call(
        functools.partial(_mm_kernel, mode=mode),
        grid=(rows // bm, n // bn),
        in_specs=in_specs,
        out_specs=pl.BlockSpec((bm, bn), lambda i, j: (i, j)),
        out_shape=jax.ShapeDtypeStruct((rows, n), out_dtype),
        compiler_params=_params(("parallel", "parallel"), vmem),
        name=name,
    )(*args)


def _ffn(u, w_up, lead, wd):
    act = _ffn_up(u, w_up, lead)
    return _matmul(act, wd, rows=act.shape[0], bm=1024, bn=512, out_dtype=F32,
                   single_buffer_x=True, name="ffn_down")


def _conv_seq_kernel(x_ref, w_ref, b_ref, o_ref, *, silu):
    x = x_ref[...].astype(F32)
    t = x.shape[0]
    row = lax.broadcasted_iota(jnp.int32, x.shape, 0)
    w = w_ref[...]
    acc = x * w[2:3] + b_ref[...]
    acc += jnp.where(row >= 2, pltpu.roll(x, 2, axis=0), 0.0) * w[0:1]
    acc += jnp.where(row >= 1, pltpu.roll(x, 1, axis=0), 0.0) * w[1:2]
    acc += jnp.where(row < t - 1, pltpu.roll(x, t - 1, axis=0), 0.0) * w[3:4]
    if silu:
        acc = acc * _sigmoid(acc)
    o_ref[...] = acc.astype(o_ref.dtype)


CONV_CB = 256


def _conv_seq(src, w, b, *, seq, first_row_block, n_seq, col0, width, silu, out_dtype, out_rows, out_row_block0,
              alias=None):
    cb0 = col0 // CONV_CB
    in_specs = [pl.BlockSpec((seq, CONV_CB), lambda s, j: (first_row_block + s, cb0 + j)),
                pl.BlockSpec((4, CONV_CB), lambda s, j: (0, j)),
                pl.BlockSpec((1, CONV_CB), lambda s, j: (0, j))]
    args = [src, w, b]
    io_alias = {}
    if alias is not None:
        in_specs.append(pl.BlockSpec(memory_space=pl.ANY))
        args.append(alias)
        io_alias = {3: 0}
    kern = functools.partial(_conv_seq_kernel, silu=silu)
    if alias is not None:
        kern = lambda x, w_, b_, a_, o: _conv_seq_kernel(x, w_, b_, o, silu=silu)
    return pl.pallas_call(
        kern,
        grid=(n_seq, width // CONV_CB),
        in_specs=in_specs,
        out_specs=pl.BlockSpec((seq, CONV_CB), lambda s, j: (out_row_block0 + s, j)),
        out_shape=jax.ShapeDtypeStruct((out_rows, width), out_dtype),
        input_output_aliases=io_alias,
        compiler_params=_params(("parallel", "parallel"), 48 << 20),
        name="conv_seq",
    )(*args)


def _split3(v):
    hi = v.astype(BF16)
    r1 = v - hi.astype(F32)
    mid = r1.astype(BF16)
    lo = (r1 - mid.astype(F32)).astype(BF16)
    return hi, mid, lo


def _dot_exact_rhs(lhs_bf16, v, terms=3):
    out = None
    for piece in _split3(v)[:terms]:
        d = jnp.dot(lhs_bf16, piece, preferred_element_type=F32)
        out = d if out is None else out + d
    return out


def _ssd_kernel(x_ref, b_ref, c_ref, dt_ref, alog_ref, y_ref, state_ref, *, rev):
    L = SSD_CHUNK
    E = SSD_HEADS_PER_GROUP
    P = SSD_HEAD_DIM
    GW = SSD_GROUP_W
    off = E if rev else 0
    s = pl.program_id(2)

    @pl.when(s == 0)
    def _():
        state_ref[...] = jnp.zeros_like(state_ref)

    ii = lax.broadcasted_iota(jnp.int32, (L, L), 0)
    jj = lax.broadcasted_iota(jnp.int32, (L, L), 1)
    tri = (jj >= ii) if rev else (jj <= ii)
    tri_b = jnp.where(tri, 1.0, 0.0).astype(BF16)
    first_half = lax.broadcasted_iota(jnp.int32, (L, 2 * P), 1) < P

    groups = []
    for gi in range(SSD_GROUPS_PER_STEP):
        tile = slice(gi * 128, (gi + 1) * 128)
        dt = dt_ref[:, tile]
        da = dt * (-jnp.exp(alog_ref[:, tile]))
        cs = _dot_exact_rhs(tri_b, da)
        tot = cs[0:1] if rev else cs[L - 1:L]
        log_end = jnp.log(dt) + (tot - cs)
        groups.append((dt, cs, tot, log_end, b_ref[:, tile], c_ref[:, tile]))

    @pl.when(s >= 2)
    def _():
        for gi, (dt, cs, tot, log_end, bm, cm) in enumerate(groups):
            cs_t = cs.T
            dt_t = dt.T
            cb = lax.dot_general(cm, bm, (((1,), (1,)), ((), ())), preferred_element_type=F32)
            for pair in range(E // 2):
                sl = slice(gi * GW + pair * 2 * P, gi * GW + (pair + 1) * 2 * P)
                xp = x_ref[:, sl]
                halves, cols = [], []
                for e in (off + 2 * pair, off + 2 * pair + 1):
                    col = jnp.broadcast_to(cs[:, e:e + 1], (L, L))
                    cols.append(col)
                    lmat = jnp.exp(jnp.where(tri, col - cs_t[e:e + 1, :], -1e30))
                    m = (cb * lmat * dt_t[e:e + 1, :]).astype(BF16)
                    halves.append(jnp.dot(m, xp, preferred_element_type=F32))
                from_start = jnp.exp(jnp.where(first_half, cols[0], cols[1]))
                y_off = jnp.dot(cm, state_ref[:, sl].astype(BF16), preferred_element_type=F32) * from_start
                y_ref[:, sl] = (jnp.where(first_half, halves[0], halves[1]) + y_off).astype(y_ref.dtype)

    for gi, (dt, cs, tot, log_end, bm, cm) in enumerate(groups):
        for pair in range(E // 2):
            sl = slice(gi * GW + pair * 2 * P, gi * GW + (pair + 1) * 2 * P)
            e0 = off + 2 * pair
            w_end = jnp.exp(jnp.where(first_half, jnp.broadcast_to(log_end[:, e0:e0 + 1], (L, 2 * P)),
                                      jnp.broadcast_to(log_end[:, e0 + 1:e0 + 2], (L, 2 * P))))
            x_end = (x_ref[:, sl].astype(F32) * w_end).astype(BF16)
            upd = lax.dot_general(bm, x_end, (((0,), (0,)), ((), ())), preferred_element_type=F32)
            tot_pair = jnp.where(first_half[0:1], jnp.broadcast_to(tot[:, e0:e0 + 1], (1, 2 * P)),
                                 jnp.broadcast_to(tot[:, e0 + 1:e0 + 2], (1, 2 * P)))
            state_ref[:, sl] = state_ref[:, sl] * jnp.exp(tot_pair) + upd


def _ssd_row_block(b, s, rev):
    n_lat = SEQ // SSD_CHUNK
    if rev:
        ctx = N_LAT // SSD_CHUNK + 2 * b + (1 - s)
        lat = n_lat * b + (n_lat + 1 - s)
    else:
        ctx = N_LAT // SSD_CHUNK + 2 * b + s
        lat = n_lat * b + (s - 2)
    return jnp.where(s < 2, ctx, lat)


def _ssd_scan(xconv, dt, alog_p, rev):
    n_lat = SEQ // SSD_CHUNK
    gps = SSD_GROUPS_PER_STEP
    rb = functools.partial(_ssd_row_block, rev=rev)

    def out_block(b, s):
        return _ssd_row_block(b, jnp.maximum(s, 2), rev)

    bw = gps * SSD_D_STATE
    b0 = SSD_D_INNER // bw
    c0 = (SSD_D_INNER + SSD_GN) // bw
    return pl.pallas_call(
        functools.partial(_ssd_kernel, rev=rev),
        grid=(BATCH, SSD_N_GROUPS // gps, n_lat + 2),
        in_specs=[pl.BlockSpec((SSD_CHUNK, gps * SSD_GROUP_W), lambda b, g, s: (rb(b, s), g)),
                  pl.BlockSpec((SSD_CHUNK, bw), lambda b, g, s: (rb(b, s), b0 + g)),
                  pl.BlockSpec((SSD_CHUNK, bw), lambda b, g, s: (rb(b, s), c0 + g)),
                  pl.BlockSpec((SSD_CHUNK, gps * 128), lambda b, g, s: (rb(b, s), g)),
                  pl.BlockSpec((1, gps * 128), lambda b, g, s: (0, g))],
        out_specs=pl.BlockSpec((SSD_CHUNK, gps * SSD_GROUP_W), lambda b, g, s: (out_block(b, s), g)),
        out_shape=jax.ShapeDtypeStruct((N_LAT, SSD_D_INNER), BF16),
        scratch_shapes=[pltpu.VMEM((SSD_D_STATE, gps * SSD_GROUP_W), F32)],
        compiler_params=_params(("parallel", "parallel", "arbitrary"), 32 << 20),
        name="ssd_scan_bwd" if rev else "ssd_scan_fwd",
    )(xconv, xconv, xconv, dt, alog_p)


def _ssd_gate_kernel(yf_ref, yb_ref, xs_ref, z_ref, d_ref, g_ref, o_ref):
    z = z_ref[...].astype(F32)
    y = yf_ref[...].astype(F32) + yb_ref[...].astype(F32) + d_ref[...] * xs_ref[...].astype(F32)
    y = y * (z * _sigmoid(z))
    o_ref[...] = (_rms(y) * g_ref[...]).astype(BF16)


def _ssd_gate(yf, yb, xconv, proj, d_exp, norm_g, bm=512):
    blk = lambda: pl.BlockSpec((bm, SSD_GROUP_W), lambda i, g: (i, g))
    vec = lambda: pl.BlockSpec((1, SSD_GROUP_W), lambda i, g: (0, g))
    return pl.pallas_call(
        _ssd_gate_kernel,
        grid=(N_LAT // bm, SSD_N_GROUPS),
        in_specs=[blk(), blk(), blk(), blk(), vec(), vec()],
        out_specs=blk(),
        out_shape=jax.ShapeDtypeStruct((N_LAT, SSD_D_INNER), BF16),
        compiler_params=_params(("parallel", "parallel"), 32 << 20),
        name="ssd_gate_norm",
    )(yf, yb, xconv, proj, d_exp, norm_g)


def _rg_conv_lat_kernel(x_ref, w_ref, b_ref, o_ref):
    w = w_ref[...]
    bias = b_ref[...]
    col = lax.broadcasted_iota(jnp.int32, (GRID_W, x_ref.shape[2]), 0)

    def row(r):
        return x_ref[r].astype(F32)

    def prev_col(v):
        return jnp.where(col >= 1, pltpu.roll(v, 1, axis=0), 0.0)

    def next_col(v):
        return jnp.where(col < GRID_W - 1, pltpu.roll(v, GRID_W - 1, axis=0), 0.0)

    for r in range(GRID_H):
        m2 = row(r - 2) if r >= 2 else prev_col(row(r - 2 + GRID_H))
        m1 = row(r - 1) if r >= 1 else prev_col(row(GRID_H - 1))
        p1 = row(r + 1) if r < GRID_H - 1 else next_col(row(0))
        o_ref[r] = m2 * w[0:1] + m1 * w[1:2] + row(r) * w[2:3] + p1 * w[3:4] + bias


def _rg_conv_lat(xr3, w, b):
    return pl.pallas_call(
        _rg_conv_lat_kernel,
        grid=(BATCH, RG_WIDTH_P // CONV_CB),
        in_specs=[pl.BlockSpec((GRID_H, GRID_W, CONV_CB), lambda bb, j: (bb, 0, j)),
                  pl.BlockSpec((4, CONV_CB), lambda bb, j: (0, j)),
                  pl.BlockSpec((1, CONV_CB), lambda bb, j: (0, j))],
        out_specs=pl.BlockSpec((GRID_H, GRID_W, CONV_CB), lambda bb, j: (bb, 0, j)),
        out_shape=jax.ShapeDtypeStruct((BATCH * GRID_H, GRID_W, RG_WIDTH_P), F32),
        compiler_params=_params(("parallel", "parallel"), 32 << 20),
        name="rg_conv_lat",
    )(xr3, w, b)


def _rg_scan_kernel(xl_ref, xc_ref, wa_ref, wx_ref, ba_ref, bx_ref, lam_ref, h_ref, carry_ref, a_ref, b_ref, *, rev):
    s = pl.program_id(2)

    @pl.when(s == 0)
    def _():
        carry_ref[...] = jnp.zeros_like(carry_ref)

    x3 = jnp.where(s == 0, xc_ref[...], xl_ref[...])
    x2 = x3.reshape(GRID_H * RG_SEG, RG_CB)
    xb = x2.astype(BF16)
    neg_c_sp = -RG_C * _softplus(-lam_ref[...])
    for k in range(RG_CB // RG_BLOCK_P):
        sl = slice(k * RG_BLOCK_P, (k + 1) * RG_BLOCK_P)
        xk = xb[:, sl]
        r = _sigmoid(jnp.dot(xk, wa_ref[k], preferred_element_type=F32) + ba_ref[:, sl])
        i = _sigmoid(jnp.dot(xk, wx_ref[k], preferred_element_type=F32) + bx_ref[:, sl])
        log_a = neg_c_sp[:, sl] * r
        a_ref[:, :, sl] = jnp.exp(log_a).reshape(GRID_H, RG_SEG, RG_BLOCK_P)
        b_ref[:, :, sl] = (jnp.sqrt(_neg_expm1(2.0 * log_a)) * (i * x2[:, sl])).reshape(GRID_H, RG_SEG, RG_BLOCK_P)

    order = range(GRID_H - 1, -1, -1) if rev else range(GRID_H)
    p = None
    for r in order:
        a = a_ref[r]
        if p is None:
            p, hloc = a, b_ref[r]
        else:
            p, hloc = a * p, a * hloc + b_ref[r]
        a_ref[r] = p
        b_ref[r] = hloc
    seg_row = lax.broadcasted_iota(jnp.int32, (RG_SEG, RG_CB), 0)
    state = carry_ref[...]
    enter = jnp.zeros((RG_SEG, RG_CB), F32)
    for c in (range(RG_SEG - 1, -1, -1) if rev else range(RG_SEG)):
        enter = jnp.where(seg_row == c, state, enter)
        state = hloc[c:c + 1] + p[c:c + 1] * state
    carry_ref[...] = state

    @pl.when(s >= 1)
    def _():
        for r in range(GRID_H):
            h_ref[r] = b_ref[r] + a_ref[r] * enter


def _rg_scan(xl3, xc3, wa, wx, ba, bx, lam, rev):
    n_segblk = GRID_W // RG_SEG

    def seg_block(s):
        return jnp.clip((n_segblk - s) if rev else (s - 1), 0, n_segblk - 1)

    blk = (GRID_H, RG_SEG, RG_CB)
    nk = RG_CB // RG_BLOCK_P
    wspec = lambda: pl.BlockSpec((nk, RG_BLOCK_P, RG_BLOCK_P), lambda cb, b, s: (cb, 0, 0))
    vspec = lambda: pl.BlockSpec((1, RG_CB), lambda cb, b, s: (0, cb))
    return pl.pallas_call(
        functools.partial(_rg_scan_kernel, rev=rev),
        grid=(RG_WIDTH_P // RG_CB, BATCH, n_segblk + 1),
        in_specs=[pl.BlockSpec(blk, lambda cb, b, s: (b, seg_block(s), cb)),
                  pl.BlockSpec(blk, lambda cb, b, s: (b, 0, cb)),
                  wspec(), wspec(), vspec(), vspec(), vspec()],
        out_specs=pl.BlockSpec(blk, lambda cb, b, s: (b, seg_block(s), cb)),
        out_shape=jax.ShapeDtypeStruct((BATCH * GRID_H, GRID_W, RG_WIDTH_P), F32),
        scratch_shapes=[pltpu.VMEM((1, RG_CB), F32), pltpu.VMEM(blk, F32), pltpu.VMEM(blk, F32)],
        compiler_params=_params(("parallel", "parallel", "arbitrary"), 40 << 20),
        name="rg_scan_bwd" if rev else "rg_scan_fwd",
    )(xl3, xc3, wa, wx, ba, bx, lam)


def _rg_gate_kernel(g_ref, hf_ref, hb_ref, o_ref):
    g = g_ref[...].astype(F32)
    o_ref[...] = (jax.nn.gelu(g) * (hf_ref[...] + hb_ref[...])).astype(BF16)


def _rg_gate(gg, hf, hb, bm=512, bn=1536):
    return pl.pallas_call(
        _rg_gate_kernel,
        grid=(N_LAT // bm, RG_WIDTH_P // bn),
        in_specs=[pl.BlockSpec((bm, bn), lambda i, j: (i, j)),
                  pl.BlockSpec((bm, bn), lambda i, j: (i, j)),
                  pl.BlockSpec((bm, bn), lambda i, j: (i, j))],
        out_specs=pl.BlockSpec((bm, bn), lambda i, j: (i, j)),
        out_shape=jax.ShapeDtypeStruct((N_LAT, RG_WIDTH_P), BF16),
        compiler_params=_params(("parallel", "parallel"), 32 << 20),
        name="rg_gate",
    )(gg, hf, hb)


def _pad_rg(a, axis):
    axis = axis % a.ndim
    shape = a.shape[:axis] + (RG_N_BLOCKS, RG_BLOCK) + a.shape[axis + 1:]
    pad = [(0, 0)] * (a.ndim + 1)
    pad[axis + 1] = (0, RG_BLOCK_P - RG_BLOCK)
    out = jnp.pad(a.reshape(shape), pad)
    return out.reshape(a.shape[:axis] + (RG_WIDTH_P,) + a.shape[axis + 1:])


def _group_lanes(a):
    lead = a.shape[:-1]
    e = SSD_HEADS_PER_GROUP
    t = a.reshape(lead + (2, SSD_N_GROUPS, e))
    t = jnp.moveaxis(t, -3, -2).reshape(lead + (SSD_N_GROUPS, 2 * e))
    t = jnp.pad(t, [(0, 0)] * len(lead) + [(0, 0), (0, 128 - 2 * e)])
    return t.reshape(lead + (DT_W,))


def kernel(x, c, ctx, c_ctx, w_ada, b_ada, norm_g, ffn_w_up, ffn_w_down, w_in, ssd_conv_w, ssd_conv_b, ssd_dt_bias,
           ssd_a_log, ssd_d, ssd_norm_g, w_ssd_out, rg_conv_w, rg_conv_b, rg_w_a, rg_b_a, rg_w_x, rg_b_x, rg_lam,
           w_rg_out, w_out):
    l = 0
    g = norm_g[l]

    wd = [ffn_w_down[l, k].astype(BF16) for k in range(2)]
    wi = w_in[l]
    w_gg = _pad_rg(wi[:, _S3:_S4].astype(BF16), 1)
    w_xr = _pad_rg(wi[:, _S4:_S5].astype(BF16), 1)
    w_dt = _group_lanes(wi[:, _S2:_S3]).astype(BF16)
    dt_bias_p = _group_lanes(ssd_dt_bias[l]).reshape(1, DT_W)
    alog_p = _group_lanes(ssd_a_log[l].reshape(2 * SSD_N_HEADS)).reshape(1, DT_W)
    d_exp = jnp.repeat(ssd_d[l], SSD_HEAD_DIM).reshape(1, SSD_D_INNER)
    ssd_ng = ssd_norm_g[l].reshape(1, SSD_D_INNER)
    ssd_cw, ssd_cb = ssd_conv_w[l], ssd_conv_b[l].reshape(1, SSD_CONV_DIM)
    rg_cw, rg_cb = _pad_rg(rg_conv_w[l], 1), _pad_rg(rg_conv_b[l], 0).reshape(1, RG_WIDTH_P)
    pad_w = lambda w: jnp.pad(w, ((0, 0), (0, RG_BLOCK_P - RG_BLOCK), (0, RG_BLOCK_P - RG_BLOCK))).astype(BF16)
    rg_wa = [pad_w(rg_w_a[l, d]) for d in range(2)]
    rg_wx = [pad_w(rg_w_x[l, d]) for d in range(2)]
    rg_ba = [_pad_rg(rg_b_a[l, d], 0).reshape(1, RG_WIDTH_P) for d in range(2)]
    rg_bx = [_pad_rg(rg_b_x[l, d], 0).reshape(1, RG_WIDTH_P) for d in range(2)]
    rg_lm = [_pad_rg(rg_lam[l, d], 0).reshape(1, RG_WIDTH_P) for d in range(2)]
    w_rg_o = _pad_rg(w_rg_out[l].astype(BF16), 0)
    w_ssd_o = w_ssd_out[l].astype(BF16)
    w_o = w_out[l].astype(BF16)

    c8 = jnp.concatenate([c, c_ctx[None], jnp.zeros((8 - BATCH - 1, D_MODEL), F32)], axis=0)
    mod3 = _ada(c8, w_ada[l], b_ada[l].reshape(1, -1)).reshape(8, 1, N_MOD * D_MODEL)

    h0 = (x.reshape(N_LAT, D_MODEL), ctx.reshape(N_CTX, D_MODEL))

    u1 = _normmod(h0, g[0], mod3, 0, N_TOK)
    y1 = _ffn(u1, ffn_w_up, (l, 0), wd[0])
    h1, u2 = _residual(h0, y1, g[1], mod3, 0, MACARON_W, nxt=(g[2], 1))

    direct = functools.partial(_matmul, u2, w_in, w_lead=(l,), bn=256, out_dtype=BF16, single_buffer_x=True)
    z = direct(rows=N_LAT, bm=N_LAT // 4, w_col0=0, n=_S1, name="in_proj_z")
    xbc = direct(rows=N_TOK, bm=N_TOK // 4, w_col0=_S1, n=SSD_CONV_DIM, name="in_proj_xbc")
    mg = direct(rows=N_LAT, bm=N_LAT // 4, w_col0=_S5, n=2 * D_MODEL, name="in_proj_merge_gates")
    proj = functools.partial(_matmul, u2, bm=1024, bn=1024)
    gg = proj(w_gg, rows=N_LAT, out_dtype=BF16, name="in_proj_gelu_gate")
    xr = proj(w_xr, rows=N_TOK, out_dtype=BF16, name="in_proj_rg")
    dt = proj(w_dt, rows=N_TOK, out_dtype=F32, softplus_bias=dt_bias_p, name="in_proj_dt")

    xconv = _conv_seq(xbc, ssd_cw, ssd_cb, seq=SEQ, first_row_block=0, n_seq=BATCH, col0=0, width=SSD_CONV_DIM,
                      silu=True, out_dtype=BF16, out_rows=N_TOK, out_row_block0=0)
    xconv = _conv_seq(xbc, ssd_cw, ssd_cb, seq=CTX_LEN, first_row_block=N_LAT // CTX_LEN, n_seq=BATCH, col0=0,
                      width=SSD_CONV_DIM, silu=True, out_dtype=BF16, out_rows=N_TOK,
                      out_row_block0=N_LAT // CTX_LEN, alias=xconv)
    y_f = _ssd_scan(xconv, dt, alog_p, rev=False)
    y_b = _ssd_scan(xconv, dt, alog_p, rev=True)
    y_n = _ssd_gate(y_f, y_b, xconv, z, d_exp, ssd_ng)

    xr_lat = _rg_conv_lat(xr.reshape(N_TOK // GRID_W, GRID_W, RG_WIDTH_P), rg_cw, rg_cb)
    xr_ctx = _conv_seq(xr, rg_cw, rg_cb, seq=CTX_LEN, first_row_block=N_LAT // CTX_LEN, n_seq=BATCH, col0=0,
                       width=RG_WIDTH_P, silu=False, out_dtype=F32, out_rows=N_CTX, out_row_block0=0)
    xr_ctx = xr_ctx.reshape(BATCH, RG_SEG, GRID_H, RG_WIDTH_P).transpose(0, 2, 1, 3)
    xr_ctx = xr_ctx.reshape(BATCH * GRID_H, RG_SEG, RG_WIDTH_P)
    h_f = _rg_scan(xr_lat, xr_ctx, rg_wa[0], rg_wx[0], rg_ba[0], rg_bx[0], rg_lm[0], rev=False)
    h_b = _rg_scan(xr_lat, xr_ctx, rg_wa[1], rg_wx[1], rg_ba[1], rg_bx[1], rg_lm[1], rev=True)
    r_in = _rg_gate(gg, h_f.reshape(N_LAT, RG_WIDTH_P), h_b.reshape(N_LAT, RG_WIDTH_P))

    m1 = _matmul(y_n, w_ssd_o, rows=N_LAT, bm=1024, bn=512, out_dtype=F32, gate=(mg, 0), single_buffer_x=True,
                 name="ssd_out_proj")
    m2 = _matmul(r_in, w_rg_o, rows=N_LAT, bm=1024, bn=512, out_dtype=BF16, gate=(mg, D_MODEL // 512), prev=m1,
                 name="rg_out_proj")
    m3 = _matmul(m2, w_o, rows=N_LAT, bm=1024, bn=512, out_dtype=F32, name="out_proj")
    h2, u3 = _residual(h1, m3, g[3], mod3, 1, 1.0, nxt=(g[4], 2))

    y3 = _ffn(u3, ffn_w_up, (l, 1), wd[1])
    out = _residual(h2, y3, g[5], mod3, 2, MACARON_W)
    return out.reshape(BATCH, SEQ, D_MODEL)
```

```python
import functools
import math

import jax
import jax.numpy as jnp
from jax import lax
from jax.experimental import pallas as pl
from jax.experimental.pallas import tpu as pltpu

F32 = jnp.float32
BF16 = jnp.bfloat16

D_MODEL = 4096
BATCH = 4
SEQ = 2048
GRID_W = 64
GRID_H = SEQ // GRID_W
CTX_LEN = 256
N_MOD = 9
EPS = 1e-6
MACARON_W = 0.5

N_LAT = BATCH * SEQ
N_CTX = BATCH * CTX_LEN
N_TOK = N_LAT + N_CTX

SSD_D_INNER = 2 * D_MODEL
SSD_HEAD_DIM = 64
SSD_N_HEADS = SSD_D_INNER // SSD_HEAD_DIM
SSD_N_GROUPS = 8
SSD_HEADS_PER_GROUP = SSD_N_HEADS // SSD_N_GROUPS
SSD_GROUP_W = SSD_D_INNER // SSD_N_GROUPS
SSD_D_STATE = 128
SSD_GN = SSD_N_GROUPS * SSD_D_STATE
SSD_CONV_DIM = SSD_D_INNER + 2 * SSD_GN
SSD_CHUNK = 128
SSD_GROUPS_PER_STEP = 4

RG_WIDTH = 5376
RG_N_BLOCKS = 16
RG_BLOCK = RG_WIDTH // RG_N_BLOCKS
RG_BLOCK_P = 384
RG_WIDTH_P = RG_N_BLOCKS * RG_BLOCK_P
RG_C = 8.0
RG_SEG = 8
RG_CB = 4 * RG_BLOCK_P

D_FF = 11008

_S1 = SSD_D_INNER
_S2 = _S1 + SSD_CONV_DIM
_S3 = _S2 + 2 * SSD_N_HEADS
_S4 = _S3 + RG_WIDTH
_S5 = _S4 + RG_WIDTH

DT_W = SSD_N_GROUPS * 128

VMEM_CAP = 56 * 1024 * 1024


def _params(sem, vmem_bytes):
    return pltpu.CompilerParams(dimension_semantics=sem,
                                vmem_limit_bytes=int(min(VMEM_CAP, max(32 << 20, vmem_bytes))))


def _softplus(x):
    return jnp.maximum(x, 0.0) + jnp.log1p(jnp.exp(-jnp.abs(x)))


def _sigmoid(x):
    return 0.5 * jnp.tanh(0.5 * x) + 0.5


def _neg_expm1_2x(x):
    t = jnp.tanh(x)
    return (t + t) / (t - 1.0)


ADA_BN = 512


def _ada_kernel(c_ref, w_ref, b_ref, o_ref):
    c = c_ref[...]
    sc = (c * _sigmoid(c)).astype(BF16)
    o_ref[...] = jnp.dot(sc, w_ref[...].astype(BF16), preferred_element_type=F32) + b_ref[...]


def _ada(c8, w_ada, b_ada):
    n = w_ada.shape[1]
    return pl.pallas_call(
        _ada_kernel,
        grid=(n // ADA_BN,),
        in_specs=[pl.BlockSpec((8, D_MODEL), lambda j: (0, 0)),
                  pl.BlockSpec((D_MODEL, ADA_BN), lambda j: (0, j)),
                  pl.BlockSpec((1, ADA_BN), lambda j: (0, j))],
        out_specs=pl.BlockSpec((8, ADA_BN), lambda j: (0, j)),
        out_shape=jax.ShapeDtypeStruct((8, n), F32),
        compiler_params=_params(("parallel",), 2 * D_MODEL * ADA_BN * 4 + D_MODEL * ADA_BN * 4 + (4 << 20)),
        name="ada_mod",
    )(c8, w_ada, b_ada)


ROW_BM = 256


def _mod_row(i):
    return jnp.where(i < N_LAT // ROW_BM, i // (SEQ // ROW_BM), BATCH)


def _mod_spec(slot):
    return pl.BlockSpec((None, 1, D_MODEL), lambda i: (_mod_row(i), 0, slot))


def _vec_spec():
    return pl.BlockSpec((1, D_MODEL), lambda i: (0, 0))


def _row_spec():
    return pl.BlockSpec((ROW_BM, D_MODEL), lambda i: (i, 0))


N_LAT_BLK = N_LAT // ROW_BM


def _token_specs(h):
    if not isinstance(h, tuple):
        return [_row_spec()], [h]
    return [pl.BlockSpec((ROW_BM, D_MODEL), lambda i: (jnp.minimum(i, N_LAT_BLK - 1), 0)),
            pl.BlockSpec((ROW_BM, D_MODEL), lambda i: (jnp.maximum(i - N_LAT_BLK, 0), 0))], list(h)


def _load_tokens(refs):
    if len(refs) == 1:
        return refs[0][...]
    return jnp.where(pl.program_id(0) < N_LAT_BLK, refs[0][...], refs[1][...])


def _rms(x):
    return x * lax.rsqrt(jnp.mean(x * x, axis=-1, keepdims=True) + EPS)


def _normmod_kernel(*refs, n_h):
    g_ref, sc_ref, sh_ref, u_ref = refs[n_h:]
    u = _rms(_load_tokens(refs[:n_h])) * g_ref[...]
    u_ref[...] = (u * (1.0 + sc_ref[...]) + sh_ref[...]).astype(BF16)


def _normmod(h, g, mod3, slot, rows):
    h_specs, h_args = _token_specs(h)
    return pl.pallas_call(
        functools.partial(_normmod_kernel, n_h=len(h_args)),
        grid=(rows // ROW_BM,),
        in_specs=h_specs + [_vec_spec(), _mod_spec(3 * slot + 1), _mod_spec(3 * slot)],
        out_specs=_row_spec(),
        out_shape=jax.ShapeDtypeStruct((rows, D_MODEL), BF16),
        compiler_params=_params(("parallel",), 40 << 20),
        name="normmod",
    )(*h_args, g.reshape(1, D_MODEL), mod3, mod3)


def _res_kernel(*refs, n_h, coef, with_next):
    y_ref, gpost_ref, gate_ref = refs[n_h:n_h + 3]
    rest = refs[n_h + 3:]
    h2 = _load_tokens(refs[:n_h]) + (coef * gate_ref[...]) * (_rms(y_ref[...]) * gpost_ref[...])
    if with_next:
        gn_ref, sc_ref, sh_ref, ho_ref, u_ref = rest
        ho_ref[...] = h2
        u = _rms(h2) * gn_ref[...]
        u_ref[...] = (u * (1.0 + sc_ref[...]) + sh_ref[...]).astype(BF16)
    else:
        (ho_ref,) = rest
        ho_ref[...] = h2


def _residual(h, y, g_post, mod3, slot, coef, nxt=None):
    rows = y.shape[0]
    h_specs, h_args = _token_specs(h)
    in_specs = h_specs + [_row_spec(), _vec_spec(), _mod_spec(3 * slot + 2)]
    args = h_args + [y, g_post.reshape(1, D_MODEL), mod3]
    out_specs = [_row_spec()]
    out_shape = [jax.ShapeDtypeStruct((rows, D_MODEL), F32)]
    if nxt is not None:
        g_next, slot_next = nxt
        in_specs += [_vec_spec(), _mod_spec(3 * slot_next + 1), _mod_spec(3 * slot_next)]
        args += [g_next.reshape(1, D_MODEL), mod3, mod3]
        out_specs.append(_row_spec())
        out_shape.append(jax.ShapeDtypeStruct((rows, D_MODEL), BF16))
    out = pl.pallas_call(
        functools.partial(_res_kernel, n_h=len(h_args), coef=coef, with_next=nxt is not None),
        grid=(rows // ROW_BM,),
        in_specs=in_specs, out_specs=out_specs, out_shape=out_shape,
        compiler_params=_params(("parallel",), 48 << 20),
        name="residual_norm",
    )(*args)
    return out if nxt is not None else out[0]


def _w_spec(lead, k, bn, col_blk0):
    squeezed = (None,) * len(lead)
    return pl.BlockSpec(squeezed + (k, bn), lambda i, j: tuple(lead) + (0, col_blk0 + j))


def _up_kernel(x_ref, wg_ref, wu_ref, o_ref):
    x = x_ref[...]
    g = jnp.dot(x, wg_ref[...].astype(BF16), preferred_element_type=F32)
    u = jnp.dot(x, wu_ref[...].astype(BF16), preferred_element_type=F32)
    o_ref[...] = (g * _sigmoid(g) * u).astype(BF16)


FFN_UP_BN = 256


def _ffn_up(u, w_up, lead):
    rows, k = u.shape
    bm, bn = rows // 4, FFN_UP_BN
    vmem = bm * k * 2 + 4 * k * bn * 4 + 2 * k * bn * 2 + 2 * bm * bn * 2 + 3 * bm * bn * 4 + (4 << 20)
    return pl.pallas_call(
        _up_kernel,
        grid=(rows // bm, D_FF // bn),
        in_specs=[pl.BlockSpec((bm, k), lambda i, j: (i, 0), pipeline_mode=pl.Buffered(1)),
                  _w_spec(lead, k, bn, 0),
                  _w_spec(lead, k, bn, D_FF // bn)],
        out_specs=pl.BlockSpec((bm, bn), lambda i, j: (i, j)),
        out_shape=jax.ShapeDtypeStruct((rows, D_FF), BF16),
        compiler_params=_params(("parallel", "parallel"), vmem),
        name="ffn_up_swiglu",
    )(u, w_up, w_up)


def _mm_kernel(x_ref, w_ref, *rest, mode):
    acc = jnp.dot(x_ref[...], w_ref[...].astype(BF16), preferred_element_type=F32)
    if mode == "plain":
        (o_ref,) = rest
    elif mode == "softplus_bias":
        b_ref, o_ref = rest
        acc = _softplus(acc + b_ref[...])
    elif mode == "gate":
        g_ref, o_ref = rest
        acc = _sigmoid(g_ref[...].astype(F32)) * acc
    else:
        g_ref, p_ref, o_ref = rest
        acc = p_ref[...] + _sigmoid(g_ref[...].astype(F32)) * acc
    o_ref[...] = acc.astype(o_ref.dtype)


def _matmul(x, w, *, rows, bm, bn, out_dtype, gate=None, prev=None, softplus_bias=None, single_buffer_x=False,
            w_lead=(), w_col0=0, n=None, name="matmul"):
    k = x.shape[1]
    n = w.shape[-1] if n is None else n
    mode = "plain" if gate is None else ("gate" if prev is None else "gate_add")
    x_kwargs = dict(pipeline_mode=pl.Buffered(1)) if single_buffer_x else {}
    in_specs = [pl.BlockSpec((bm, k), lambda i, j: (i, 0), **x_kwargs),
                _w_spec(w_lead, k, bn, w_col0 // bn)]
    args = [x, w]
    if softplus_bias is not None:
        mode = "softplus_bias"
        in_specs.append(pl.BlockSpec((1, bn), lambda i, j: (0, j)))
        args.append(softplus_bias)
    if gate is not None:
        g_arr, g_blk0 = gate
        in_specs.append(pl.BlockSpec((bm, bn), lambda i, j: (i, g_blk0 + j)))
        args.append(g_arr)
    if prev is not None:
        in_specs.append(pl.BlockSpec((bm, bn), lambda i, j: (i, j)))
        args.append(prev)
    xbuf = 1 if single_buffer_x else 2
    w_bytes = 2 * k * bn * w.dtype.itemsize + (k * bn * 2 if w.dtype != BF16 else 0)
    vmem = xbuf * bm * k * 2 + w_bytes + 5 * bm * bn * 4 + (4 << 20)
    return pl.pallas_call(
        functools.partial(_mm_kernel, mode=mode),
        grid=(rows // bm, n // bn),
        in_specs=in_specs,
        out_specs=pl.BlockSpec((bm, bn), lambda i, j: (i, j)),
        out_shape=jax.ShapeDtypeStruct((rows, n), out_dtype),
        compiler_params=_params(("parallel", "parallel"), vmem),
        name=name,
    )(*args)


def _ffn(u, w_up, w_down, lead):
    act = _ffn_up(u, w_up, lead)
    return _matmul(act, w_down, w_lead=lead, rows=act.shape[0], bm=1024, bn=512, out_dtype=F32,
                   single_buffer_x=True, name="ffn_down")


def _conv_seq_kernel(x_ref, w_ref, b_ref, o_ref, *, silu):
    x = x_ref[...].astype(F32)
    t = x.shape[0]
    row = lax.broadcasted_iota(jnp.int32, x.shape, 0)
    w = w_ref[...]
    acc = x * w[2:3] + b_ref[...]
    acc += jnp.where(row >= 2, pltpu.roll(x, 2, axis=0), 0.0) * w[0:1]
    acc += jnp.where(row >= 1, pltpu.roll(x, 1, axis=0), 0.0) * w[1:2]
    acc += jnp.where(row < t - 1, pltpu.roll(x, t - 1, axis=0), 0.0) * w[3:4]
    if silu:
        acc = acc * _sigmoid(acc)
    o_ref[...] = acc.astype(o_ref.dtype)


CONV_CB = 256


def _conv_seq(src, w, b, *, seq, first_row_block, n_seq, col0, width, silu, out_dtype, out_rows, out_row_block0,
              alias=None):
    cb0 = col0 // CONV_CB
    in_specs = [pl.BlockSpec((seq, CONV_CB), lambda s, j: (first_row_block + s, cb0 + j)),
                pl.BlockSpec((4, CONV_CB), lambda s, j: (0, j)),
                pl.BlockSpec((1, CONV_CB), lambda s, j: (0, j))]
    args = [src, w, b]
    io_alias = {}
    if alias is not None:
        in_specs.append(pl.BlockSpec(memory_space=pl.ANY))
        args.append(alias)
        io_alias = {3: 0}
    kern = functools.partial(_conv_seq_kernel, silu=silu)
    if alias is not None:
        kern = lambda x, w_, b_, a_, o: _conv_seq_kernel(x, w_, b_, o, silu=silu)
    return pl.pallas_call(
        kern,
        grid=(n_seq, width // CONV_CB),
        in_specs=in_specs,
        out_specs=pl.BlockSpec((seq, CONV_CB), lambda s, j: (out_row_block0 + s, j)),
        out_shape=jax.ShapeDtypeStruct((out_rows, width), out_dtype),
        input_output_aliases=io_alias,
        compiler_params=_params(("parallel", "parallel"), 48 << 20),
        name="conv_seq",
    )(*args)


def _split3(v):
    hi = v.astype(BF16)
    r1 = v - hi.astype(F32)
    mid = r1.astype(BF16)
    lo = (r1 - mid.astype(F32)).astype(BF16)
    return hi, mid, lo


def _dot_exact_rhs(lhs_bf16, v, terms=3):
    out = None
    for piece in _split3(v)[:terms]:
        d = jnp.dot(lhs_bf16, piece, preferred_element_type=F32)
        out = d if out is None else out + d
    return out


def _ssd_kernel(x_ref, b_ref, c_ref, dt_ref, alog_ref, y_ref, state_ref, *, rev):
    L = SSD_CHUNK
    E = SSD_HEADS_PER_GROUP
    P = SSD_HEAD_DIM
    GW = SSD_GROUP_W
    off = E if rev else 0
    s = pl.program_id(2)

    @pl.when(s == 0)
    def _():
        state_ref[...] = jnp.zeros_like(state_ref)

    ii = lax.broadcasted_iota(jnp.int32, (L, L), 0)
    jj = lax.broadcasted_iota(jnp.int32, (L, L), 1)
    tri = (jj >= ii) if rev else (jj <= ii)
    tri_b = jnp.where(tri, 1.0, 0.0).astype(BF16)
    first_half = lax.broadcasted_iota(jnp.int32, (L, 2 * P), 1) < P

    groups = []
    for gi in range(SSD_GROUPS_PER_STEP):
        tile = slice(gi * 128, (gi + 1) * 128)
        dt = dt_ref[:, tile]
        da = dt * (-jnp.exp(alog_ref[:, tile]))
        cs = _dot_exact_rhs(tri_b, da)
        tot = cs[0:1] if rev else cs[L - 1:L]
        log_end = jnp.log(dt) + (tot - cs)
        groups.append((dt, cs, tot, log_end, b_ref[:, tile], c_ref[:, tile]))

    @pl.when(s >= 2)
    def _():
        for gi, (dt, cs, tot, log_end, bm, cm) in enumerate(groups):
            cs_t = cs.T
            dt_t = dt.T
            cb = lax.dot_general(cm, bm, (((1,), (1,)), ((), ())), preferred_element_type=F32)
            for pair in range(E // 2):
                sl = slice(gi * GW + pair * 2 * P, gi * GW + (pair + 1) * 2 * P)
                xp = x_ref[:, sl]
                halves, cols = [], []
                for e in (off + 2 * pair, off + 2 * pair + 1):
                    col = jnp.broadcast_to(cs[:, e:e + 1], (L, L))
                    cols.append(col)
                    lmat = jnp.exp(jnp.where(tri, col - cs_t[e:e + 1, :], -1e30))
                    m = (cb * lmat * dt_t[e:e + 1, :]).astype(BF16)
                    halves.append(jnp.dot(m, xp, preferred_element_type=F32))
                from_start = jnp.exp(jnp.where(first_half, cols[0], cols[1]))
                y_off = jnp.dot(cm, state_ref[:, sl].astype(BF16), preferred_element_type=F32) * from_start
                y_ref[:, sl] = (jnp.where(first_half, halves[0], halves[1]) + y_off).astype(y_ref.dtype)

    for gi, (dt, cs, tot, log_end, bm, cm) in enumerate(groups):
        for pair in range(E // 2):
            sl = slice(gi * GW + pair * 2 * P, gi * GW + (pair + 1) * 2 * P)
            e0 = off + 2 * pair
            w_end = jnp.exp(jnp.where(first_half, jnp.broadcast_to(log_end[:, e0:e0 + 1], (L, 2 * P)),
                                      jnp.broadcast_to(log_end[:, e0 + 1:e0 + 2], (L, 2 * P))))
            x_end = (x_ref[:, sl].astype(F32) * w_end).astype(BF16)
            upd = lax.dot_general(bm, x_end, (((0,), (0,)), ((), ())), preferred_element_type=F32)
            tot_pair = jnp.where(first_half[0:1], jnp.broadcast_to(tot[:, e0:e0 + 1], (1, 2 * P)),
                                 jnp.broadcast_to(tot[:, e0 + 1:e0 + 2], (1, 2 * P)))
            state_ref[:, sl] = state_ref[:, sl] * jnp.exp(tot_pair) + upd


def _ssd_row_block(b, s, rev):
    n_lat = SEQ // SSD_CHUNK
    if rev:
        ctx = N_LAT // SSD_CHUNK + 2 * b + (1 - s)
        lat = n_lat * b + (n_lat + 1 - s)
    else:
        ctx = N_LAT // SSD_CHUNK + 2 * b + s
        lat = n_lat * b + (s - 2)
    return jnp.where(s < 2, ctx, lat)


def _ssd_scan(xconv, dt, alog_p, rev):
    n_lat = SEQ // SSD_CHUNK
    gps = SSD_GROUPS_PER_STEP
    rb = functools.partial(_ssd_row_block, rev=rev)

    def out_block(b, s):
        return _ssd_row_block(b, jnp.maximum(s, 2), rev)

    bw = gps * SSD_D_STATE
    b0 = SSD_D_INNER // bw
    c0 = (SSD_D_INNER + SSD_GN) // bw
    return pl.pallas_call(
        functools.partial(_ssd_kernel, rev=rev),
        grid=(BATCH, SSD_N_GROUPS // gps, n_lat + 2),
        in_specs=[pl.BlockSpec((SSD_CHUNK, gps * SSD_GROUP_W), lambda b, g, s: (rb(b, s), g)),
                  pl.BlockSpec((SSD_CHUNK, bw), lambda b, g, s: (rb(b, s), b0 + g)),
                  pl.BlockSpec((SSD_CHUNK, bw), lambda b, g, s: (rb(b, s), c0 + g)),
                  pl.BlockSpec((SSD_CHUNK, gps * 128), lambda b, g, s: (rb(b, s), g)),
                  pl.BlockSpec((1, gps * 128), lambda b, g, s: (0, g))],
        out_specs=pl.BlockSpec((SSD_CHUNK, gps * SSD_GROUP_W), lambda b, g, s: (out_block(b, s), g)),
        out_shape=jax.ShapeDtypeStruct((N_LAT, SSD_D_INNER), BF16),
        scratch_shapes=[pltpu.VMEM((SSD_D_STATE, gps * SSD_GROUP_W), F32)],
        compiler_params=_params(("parallel", "parallel", "arbitrary"), 32 << 20),
        name="ssd_scan_bwd" if rev else "ssd_scan_fwd",
    )(xconv, xconv, xconv, dt, alog_p)


def _ssd_gate_kernel(yf_ref, yb_ref, xs_ref, z_ref, d_ref, g_ref, o_ref):
    z = z_ref[...].astype(F32)
    y = yf_ref[...].astype(F32) + yb_ref[...].astype(F32) + d_ref[...] * xs_ref[...].astype(F32)
    y = y * (z * _sigmoid(z))
    o_ref[...] = (_rms(y) * g_ref[...]).astype(BF16)


def _ssd_gate(yf, yb, xconv, proj, d_exp, norm_g, bm=512):
    blk = lambda: pl.BlockSpec((bm, SSD_GROUP_W), lambda i, g: (i, g))
    vec = lambda: pl.BlockSpec((1, SSD_GROUP_W), lambda i, g: (0, g))
    return pl.pallas_call(
        _ssd_gate_kernel,
        grid=(N_LAT // bm, SSD_N_GROUPS),
        in_specs=[blk(), blk(), blk(), blk(), vec(), vec()],
        out_specs=blk(),
        out_shape=jax.ShapeDtypeStruct((N_LAT, SSD_D_INNER), BF16),
        compiler_params=_params(("parallel", "parallel"), 32 << 20),
        name="ssd_gate_norm",
    )(yf, yb, xconv, proj, d_exp, norm_g)


def _rg_conv_lat_kernel(x_ref, w_ref, b_ref, o_ref):
    w = w_ref[...]
    bias = b_ref[...]
    col = lax.broadcasted_iota(jnp.int32, (GRID_W, x_ref.shape[2]), 0)

    def row(r):
        return x_ref[r].astype(F32)

    def prev_col(v):
        return jnp.where(col >= 1, pltpu.roll(v, 1, axis=0), 0.0)

    def next_col(v):
        return jnp.where(col < GRID_W - 1, pltpu.roll(v, GRID_W - 1, axis=0), 0.0)

    for r in range(GRID_H):
        m2 = row(r - 2) if r >= 2 else prev_col(row(r - 2 + GRID_H))
        m1 = row(r - 1) if r >= 1 else prev_col(row(GRID_H - 1))
        p1 = row(r + 1) if r < GRID_H - 1 else next_col(row(0))
        o_ref[r] = m2 * w[0:1] + m1 * w[1:2] + row(r) * w[2:3] + p1 * w[3:4] + bias


def _rg_conv_lat(xr3, w, b):
    return pl.pallas_call(
        _rg_conv_lat_kernel,
        grid=(BATCH, RG_WIDTH_P // CONV_CB),
        in_specs=[pl.BlockSpec((GRID_H, GRID_W, CONV_CB), lambda bb, j: (bb, 0, j)),
                  pl.BlockSpec((4, CONV_CB), lambda bb, j: (0, j)),
                  pl.BlockSpec((1, CONV_CB), lambda bb, j: (0, j))],
        out_specs=pl.BlockSpec((GRID_H, GRID_W, CONV_CB), lambda bb, j: (bb, 0, j)),
        out_shape=jax.ShapeDtypeStruct((BATCH * GRID_H, GRID_W, RG_WIDTH_P), F32),
        compiler_params=_params(("parallel", "parallel"), 32 << 20),
        name="rg_conv_lat",
    )(xr3, w, b)


def _rg_scan_kernel(xl_ref, xc_ref, wa_ref, wx_ref, ba_ref, bx_ref, lam_ref, h_ref, carry_ref, a_ref, b_ref, *, rev):
    s = pl.program_id(2)

    @pl.when(s == 0)
    def _():
        carry_ref[...] = jnp.zeros_like(carry_ref)

    x3 = jnp.where(s == 0, xc_ref[...], xl_ref[...])
    x2 = x3.reshape(GRID_H * RG_SEG, RG_CB)
    xb = x2.astype(BF16)
    neg_c_sp = -RG_C * _softplus(-lam_ref[...])
    for k in range(RG_CB // RG_BLOCK_P):
        sl = slice(k * RG_BLOCK_P, (k + 1) * RG_BLOCK_P)
        xk = xb[:, sl]
        r = _sigmoid(jnp.dot(xk, wa_ref[k], preferred_element_type=F32) + ba_ref[:, sl])
        i = _sigmoid(jnp.dot(xk, wx_ref[k], preferred_element_type=F32) + bx_ref[:, sl])
        log_a = neg_c_sp[:, sl] * r
        a_ref[:, :, sl] = jnp.exp(log_a).reshape(GRID_H, RG_SEG, RG_BLOCK_P)
        b_ref[:, :, sl] = (jnp.sqrt(_neg_expm1_2x(log_a)) * (i * x2[:, sl])).reshape(GRID_H, RG_SEG, RG_BLOCK_P)

    order = range(GRID_H - 1, -1, -1) if rev else range(GRID_H)
    p = None
    for r in order:
        a = a_ref[r]
        if p is None:
            p, hloc = a, b_ref[r]
        else:
            p, hloc = a * p, a * hloc + b_ref[r]
        a_ref[r] = p
        b_ref[r] = hloc
    seg_row = lax.broadcasted_iota(jnp.int32, (RG_SEG, RG_CB), 0)
    state = carry_ref[...]
    enter = jnp.zeros((RG_SEG, RG_CB), F32)
    for c in (range(RG_SEG - 1, -1, -1) if rev else range(RG_SEG)):
        enter = jnp.where(seg_row == c, state, enter)
        state = hloc[c:c + 1] + p[c:c + 1] * state
    carry_ref[...] = state

    @pl.when(s >= 1)
    def _():
        for r in range(GRID_H):
            h_ref[r] = b_ref[r] + a_ref[r] * enter


def _rg_scan(xl3, xc3, wa, wx, ba, bx, lam, rev):
    n_segblk = GRID_W // RG_SEG

    def seg_block(s):
        return jnp.clip((n_segblk - s) if rev else (s - 1), 0, n_segblk - 1)

    blk = (GRID_H, RG_SEG, RG_CB)
    nk = RG_CB // RG_BLOCK_P
    wspec = lambda: pl.BlockSpec((nk, RG_BLOCK_P, RG_BLOCK_P), lambda cb, b, s: (cb, 0, 0))
    vspec = lambda: pl.BlockSpec((1, RG_CB), lambda cb, b, s: (0, cb))
    return pl.pallas_call(
        functools.partial(_rg_scan_kernel, rev=rev),
        grid=(RG_WIDTH_P // RG_CB, BATCH, n_segblk + 1),
        in_specs=[pl.BlockSpec(blk, lambda cb, b, s: (b, seg_block(s), cb)),
                  pl.BlockSpec(blk, lambda cb, b, s: (b, 0, cb)),
                  wspec(), wspec(), vspec(), vspec(), vspec()],
        out_specs=pl.BlockSpec(blk, lambda cb, b, s: (b, seg_block(s), cb)),
        out_shape=jax.ShapeDtypeStruct((BATCH * GRID_H, GRID_W, RG_WIDTH_P), F32),
        scratch_shapes=[pltpu.VMEM((1, RG_CB), F32), pltpu.VMEM(blk, F32), pltpu.VMEM(blk, F32)],
        compiler_params=_params(("parallel", "parallel", "arbitrary"), 40 << 20),
        name="rg_scan_bwd" if rev else "rg_scan_fwd",
    )(xl3, xc3, wa, wx, ba, bx, lam)


def _rg_gate_kernel(g_ref, hf_ref, hb_ref, o_ref):
    g = g_ref[...].astype(F32)
    o_ref[...] = (jax.nn.gelu(g) * (hf_ref[...] + hb_ref[...])).astype(BF16)


def _rg_gate(gg, hf, hb, bm=512, bn=1536):
    return pl.pallas_call(
        _rg_gate_kernel,
        grid=(N_LAT // bm, RG_WIDTH_P // bn),
        in_specs=[pl.BlockSpec((bm, bn), lambda i, j: (i, j)),
                  pl.BlockSpec((bm, bn), lambda i, j: (i, j)),
                  pl.BlockSpec((bm, bn), lambda i, j: (i, j))],
        out_specs=pl.BlockSpec((bm, bn), lambda i, j: (i, j)),
        out_shape=jax.ShapeDtypeStruct((N_LAT, RG_WIDTH_P), BF16),
        compiler_params=_params(("parallel", "parallel"), 32 << 20),
        name="rg_gate",
    )(gg, hf, hb)


def _pad_rg(a, axis):
    axis = axis % a.ndim
    shape = a.shape[:axis] + (RG_N_BLOCKS, RG_BLOCK) + a.shape[axis + 1:]
    pad = [(0, 0)] * (a.ndim + 1)
    pad[axis + 1] = (0, RG_BLOCK_P - RG_BLOCK)
    out = jnp.pad(a.reshape(shape), pad)
    return out.reshape(a.shape[:axis] + (RG_WIDTH_P,) + a.shape[axis + 1:])


def _group_lanes(a):
    lead = a.shape[:-1]
    e = SSD_HEADS_PER_GROUP
    t = a.reshape(lead + (2, SSD_N_GROUPS, e))
    t = jnp.moveaxis(t, -3, -2).reshape(lead + (SSD_N_GROUPS, 2 * e))
    t = jnp.pad(t, [(0, 0)] * len(lead) + [(0, 0), (0, 128 - 2 * e)])
    return t.reshape(lead + (DT_W,))


def kernel(x, c, ctx, c_ctx, w_ada, b_ada, norm_g, ffn_w_up, ffn_w_down, w_in, ssd_conv_w, ssd_conv_b, ssd_dt_bias,
           ssd_a_log, ssd_d, ssd_norm_g, w_ssd_out, rg_conv_w, rg_conv_b, rg_w_a, rg_b_a, rg_w_x, rg_b_x, rg_lam,
           w_rg_out, w_out):
    l = 0
    g = norm_g[l]

    wd = ffn_w_down.astype(BF16)
    wi = w_in[l]
    wi_tail = lax.optimization_barrier(wi[:, _S2:_S5])
    w_dt = _group_lanes(wi_tail[:, :_S3 - _S2]).astype(BF16)
    w_gg = _pad_rg(wi_tail[:, _S3 - _S2:_S4 - _S2].astype(BF16), 1)
    w_xr = _pad_rg(wi_tail[:, _S4 - _S2:].astype(BF16), 1)
    dt_bias_p = _group_lanes(ssd_dt_bias[l]).reshape(1, DT_W)
    alog_p = _group_lanes(ssd_a_log[l].reshape(2 * SSD_N_HEADS)).reshape(1, DT_W)
    d_exp = jnp.repeat(ssd_d[l], SSD_HEAD_DIM).reshape(1, SSD_D_INNER)
    ssd_ng = ssd_norm_g[l].reshape(1, SSD_D_INNER)
    ssd_cw, ssd_cb = ssd_conv_w[l], ssd_conv_b[l].reshape(1, SSD_CONV_DIM)
    rg_cw, rg_cb = _pad_rg(rg_conv_w[l], 1), _pad_rg(rg_conv_b[l], 0).reshape(1, RG_WIDTH_P)
    pad_w = lambda w: jnp.pad(w, ((0, 0), (0, RG_BLOCK_P - RG_BLOCK), (0, RG_BLOCK_P - RG_BLOCK))).astype(BF16)
    rg_wa = [pad_w(rg_w_a[l, d]) for d in range(2)]
    rg_wx = [pad_w(rg_w_x[l, d]) for d in range(2)]
    rg_ba = [_pad_rg(rg_b_a[l, d], 0).reshape(1, RG_WIDTH_P) for d in range(2)]
    rg_bx = [_pad_rg(rg_b_x[l, d], 0).reshape(1, RG_WIDTH_P) for d in range(2)]
    rg_lm = [_pad_rg(rg_lam[l, d], 0).reshape(1, RG_WIDTH_P) for d in range(2)]
    w_rg_o = _pad_rg(w_rg_out[l].astype(BF16), 0)
    w_ssd_o = w_ssd_out[l].astype(BF16)
    w_o = w_out[l].astype(BF16)

    c8 = jnp.concatenate([c, c_ctx[None], jnp.zeros((8 - BATCH - 1, D_MODEL), F32)], axis=0)
    mod3 = _ada(c8, w_ada[l], b_ada[l].reshape(1, -1)).reshape(8, 1, N_MOD * D_MODEL)

    h0 = (x.reshape(N_LAT, D_MODEL), ctx.reshape(N_CTX, D_MODEL))

    u1 = _normmod(h0, g[0], mod3, 0, N_TOK)
    y1 = _ffn(u1, ffn_w_up, wd, (l, 0))
    h1, u2 = _residual(h0, y1, g[1], mod3, 0, MACARON_W, nxt=(g[2], 1))

    direct = functools.partial(_matmul, u2, w_in, w_lead=(l,), out_dtype=BF16, single_buffer_x=True)
    z = direct(rows=N_LAT, bm=N_LAT // 4, bn=512, w_col0=0, n=_S1, name="in_proj_z")
    xbc = direct(rows=N_TOK, bm=N_TOK // 4, bn=512, w_col0=_S1, n=SSD_CONV_DIM, name="in_proj_xbc")
    mg = direct(rows=N_LAT, bm=N_LAT // 4, bn=256, w_col0=_S5, n=2 * D_MODEL, name="in_proj_merge_gates")
    proj = functools.partial(_matmul, u2, bm=1024, bn=1024)
    gg = proj(w_gg, rows=N_LAT, out_dtype=BF16, name="in_proj_gelu_gate")
    xr = proj(w_xr, rows=N_TOK, out_dtype=BF16, name="in_proj_rg")
    dt = proj(w_dt, rows=N_TOK, out_dtype=F32, softplus_bias=dt_bias_p, name="in_proj_dt")

    xconv = _conv_seq(xbc, ssd_cw, ssd_cb, seq=SEQ, first_row_block=0, n_seq=BATCH, col0=0, width=SSD_CONV_DIM,
                      silu=True, out_dtype=BF16, out_rows=N_TOK, out_row_block0=0)
    xconv = _conv_seq(xbc, ssd_cw, ssd_cb, seq=CTX_LEN, first_row_block=N_LAT // CTX_LEN, n_seq=BATCH, col0=0,
                      width=SSD_CONV_DIM, silu=True, out_dtype=BF16, out_rows=N_TOK,
                      out_row_block0=N_LAT // CTX_LEN, alias=xconv)
    y_f = _ssd_scan(xconv, dt, alog_p, rev=False)
    y_b = _ssd_scan(xconv, dt, alog_p, rev=True)
    y_n = _ssd_gate(y_f, y_b, xconv, z, d_exp, ssd_ng)

    xr_lat = _rg_conv_lat(xr.reshape(N_TOK // GRID_W, GRID_W, RG_WIDTH_P), rg_cw, rg_cb)
    xr_ctx = _conv_seq(xr, rg_cw, rg_cb, seq=CTX_LEN, first_row_block=N_LAT // CTX_LEN, n_seq=BATCH, col0=0,
                       width=RG_WIDTH_P, silu=False, out_dtype=F32, out_rows=N_CTX, out_row_block0=0)
    xr_ctx = xr_ctx.reshape(BATCH, RG_SEG, GRID_H, RG_WIDTH_P).transpose(0, 2, 1, 3)
    xr_ctx = xr_ctx.reshape(BATCH * GRID_H, RG_SEG, RG_WIDTH_P)
    h_f = _rg_scan(xr_lat, xr_ctx, rg_wa[0], rg_wx[0], rg_ba[0], rg_bx[0], rg_lm[0], rev=False)
    h_b = _rg_scan(xr_lat, xr_ctx, rg_wa[1], rg_wx[1], rg_ba[1], rg_bx[1], rg_lm[1], rev=True)
    r_in = _rg_gate(gg, h_f.reshape(N_LAT, RG_WIDTH_P), h_b.reshape(N_LAT, RG_WIDTH_P))

    m1 = _matmul(y_n, w_ssd_o, rows=N_LAT, bm=1024, bn=512, out_dtype=F32, gate=(mg, 0), single_buffer_x=True,
                 name="ssd_out_proj")
    m2 = _matmul(r_in, w_rg_o, rows=N_LAT, bm=1024, bn=512, out_dtype=BF16, gate=(mg, D_MODEL // 512), prev=m1,
                 name="rg_out_proj")
    m3 = _matmul(m2, w_o, rows=N_LAT, bm=1024, bn=512, out_dtype=F32, name="out_proj")
    h2, u3 = _residual(h1, m3, g[3], mod3, 1, 1.0, nxt=(g[4], 2))

    y3 = _ffn(u3, ffn_w_up, wd, (l, 1))
    out = _residual(h2, y3, g[5], mod3, 2, MACARON_W)
    return out.reshape(BATCH, SEQ, D_MODEL)
```

```python
import functools
import math

import jax
import jax.numpy as jnp
from jax import lax
from jax.experimental import pallas as pl
from jax.experimental.pallas import tpu as pltpu

F32 = jnp.float32
BF16 = jnp.bfloat16

D_MODEL = 4096
BATCH = 4
SEQ = 2048
GRID_W = 64
GRID_H = SEQ // GRID_W
CTX_LEN = 256
N_MOD = 9
EPS = 1e-6
MACARON_W = 0.5

N_LAT = BATCH * SEQ
N_CTX = BATCH * CTX_LEN
N_TOK = N_LAT + N_CTX

SSD_D_INNER = 2 * D_MODEL
SSD_HEAD_DIM = 64
SSD_N_HEADS = SSD_D_INNER // SSD_HEAD_DIM
SSD_N_GROUPS = 8
SSD_HEADS_PER_GROUP = SSD_N_HEADS // SSD_N_GROUPS
SSD_GROUP_W = SSD_D_INNER // SSD_N_GROUPS
SSD_D_STATE = 128
SSD_GN = SSD_N_GROUPS * SSD_D_STATE
SSD_CONV_DIM = SSD_D_INNER + 2 * SSD_GN
SSD_CHUNK = 128
SSD_GROUPS_PER_STEP = 4

RG_WIDTH = 5376
RG_N_BLOCKS = 16
RG_BLOCK = RG_WIDTH // RG_N_BLOCKS
RG_BLOCK_P = 384
RG_WIDTH_P = RG_N_BLOCKS * RG_BLOCK_P
RG_C = 8.0
RG_SEG = 8
RG_CB = 4 * RG_BLOCK_P

D_FF = 11008

_S1 = SSD_D_INNER
_S2 = _S1 + SSD_CONV_DIM
_S3 = _S2 + 2 * SSD_N_HEADS
_S4 = _S3 + RG_WIDTH
_S5 = _S4 + RG_WIDTH

DT_W = SSD_N_GROUPS * 128

VMEM_CAP = 56 * 1024 * 1024


def _params(sem, vmem_bytes):
    return pltpu.CompilerParams(dimension_semantics=sem,
                                vmem_limit_bytes=int(min(VMEM_CAP, max(32 << 20, vmem_bytes))))


def _softplus(x):
    return jnp.maximum(x, 0.0) + jnp.log1p(jnp.exp(-jnp.abs(x)))


def _sigmoid(x):
    return 0.5 * jnp.tanh(0.5 * x) + 0.5


def _neg_expm1_2x(x):
    t = jnp.tanh(x)
    return (t + t) / (t - 1.0)


ADA_BN = 512


def _ada_kernel(c_ref, w_ref, b_ref, o_ref):
    c = c_ref[...]
    sc = (c * _sigmoid(c)).astype(BF16)
    o_ref[...] = jnp.dot(sc, w_ref[...].astype(BF16), preferred_element_type=F32) + b_ref[...]


def _ada(c8, w_ada, b_ada):
    n = w_ada.shape[1]
    return pl.pallas_call(
        _ada_kernel,
        grid=(n // ADA_BN,),
        in_specs=[pl.BlockSpec((8, D_MODEL), lambda j: (0, 0)),
                  pl.BlockSpec((D_MODEL, ADA_BN), lambda j: (0, j)),
                  pl.BlockSpec((1, ADA_BN), lambda j: (0, j))],
        out_specs=pl.BlockSpec((8, ADA_BN), lambda j: (0, j)),
        out_shape=jax.ShapeDtypeStruct((8, n), F32),
        compiler_params=_params(("parallel",), 2 * D_MODEL * ADA_BN * 4 + D_MODEL * ADA_BN * 4 + (4 << 20)),
        name="ada_mod",
    )(c8, w_ada, b_ada)


ROW_BM = 256


def _mod_row(i):
    return jnp.where(i < N_LAT // ROW_BM, i // (SEQ // ROW_BM), BATCH)


def _mod_spec(slot):
    return pl.BlockSpec((None, 1, D_MODEL), lambda i: (_mod_row(i), 0, slot))


def _vec_spec():
    return pl.BlockSpec((1, D_MODEL), lambda i: (0, 0))


def _row_spec():
    return pl.BlockSpec((ROW_BM, D_MODEL), lambda i: (i, 0))


N_LAT_BLK = N_LAT // ROW_BM


def _token_specs(h):
    if not isinstance(h, tuple):
        return [_row_spec()], [h]
    return [pl.BlockSpec((ROW_BM, D_MODEL), lambda i: (jnp.minimum(i, N_LAT_BLK - 1), 0)),
            pl.BlockSpec((ROW_BM, D_MODEL), lambda i: (jnp.maximum(i - N_LAT_BLK, 0), 0))], list(h)


def _load_tokens(refs):
    if len(refs) == 1:
        return refs[0][...]
    return jnp.where(pl.program_id(0) < N_LAT_BLK, refs[0][...], refs[1][...])


def _rms(x):
    return x * lax.rsqrt(jnp.mean(x * x, axis=-1, keepdims=True) + EPS)


def _normmod_kernel(*refs, n_h):
    g_ref, sc_ref, sh_ref, u_ref = refs[n_h:]
    u = _rms(_load_tokens(refs[:n_h])) * g_ref[...]
    u_ref[...] = (u * (1.0 + sc_ref[...]) + sh_ref[...]).astype(BF16)


def _normmod(h, g, mod3, slot, rows):
    h_specs, h_args = _token_specs(h)
    return pl.pallas_call(
        functools.partial(_normmod_kernel, n_h=len(h_args)),
        grid=(rows // ROW_BM,),
        in_specs=h_specs + [_vec_spec(), _mod_spec(3 * slot + 1), _mod_spec(3 * slot)],
        out_specs=_row_spec(),
        out_shape=jax.ShapeDtypeStruct((rows, D_MODEL), BF16),
        compiler_params=_params(("parallel",), 40 << 20),
        name="normmod",
    )(*h_args, g.reshape(1, D_MODEL), mod3, mod3)


def _res_kernel(*refs, n_h, coef, with_next):
    y_ref, gpost_ref, gate_ref = refs[n_h:n_h + 3]
    rest = refs[n_h + 3:]
    h2 = _load_tokens(refs[:n_h]) + (coef * gate_ref[...]) * (_rms(y_ref[...]) * gpost_ref[...])
    if with_next:
        gn_ref, sc_ref, sh_ref, ho_ref, u_ref = rest
        ho_ref[...] = h2
        u = _rms(h2) * gn_ref[...]
        u_ref[...] = (u * (1.0 + sc_ref[...]) + sh_ref[...]).astype(BF16)
    else:
        (ho_ref,) = rest
        ho_ref[...] = h2


def _residual(h, y, g_post, mod3, slot, coef, nxt=None):
    rows = y.shape[0]
    h_specs, h_args = _token_specs(h)
    in_specs = h_specs + [_row_spec(), _vec_spec(), _mod_spec(3 * slot + 2)]
    args = h_args + [y, g_post.reshape(1, D_MODEL), mod3]
    out_specs = [_row_spec()]
    out_shape = [jax.ShapeDtypeStruct((rows, D_MODEL), F32)]
    if nxt is not None:
        g_next, slot_next = nxt
        in_specs += [_vec_spec(), _mod_spec(3 * slot_next + 1), _mod_spec(3 * slot_next)]
        args += [g_next.reshape(1, D_MODEL), mod3, mod3]
        out_specs.append(_row_spec())
        out_shape.append(jax.ShapeDtypeStruct((rows, D_MODEL), BF16))
    out = pl.pallas_call(
        functools.partial(_res_kernel, n_h=len(h_args), coef=coef, with_next=nxt is not None),
        grid=(rows // ROW_BM,),
        in_specs=in_specs, out_specs=out_specs, out_shape=out_shape,
        compiler_params=_params(("parallel",), 48 << 20),
        name="residual_norm",
    )(*args)
    return out if nxt is not None else out[0]


def _w_spec(lead, k, bn, col_blk0):
    squeezed = (None,) * len(lead)
    return pl.BlockSpec(squeezed + (k, bn), lambda i, j: tuple(lead) + (0, col_blk0 + j))


def _side_cast_specs(side, n_col_blocks):
    arr, lead = side
    rows, cols = arr.shape[-2:]
    r = rows // n_col_blocks
    squeezed = (None,) * len(lead)
    in_spec = pl.BlockSpec(squeezed + (r, cols), lambda i, j: tuple(lead) + (j, 0))
    out_spec = pl.BlockSpec((r, cols), lambda i, j: (j, 0))
    return in_spec, out_spec, jax.ShapeDtypeStruct((rows, cols), BF16), 2 * r * cols * (4 + 2)


def _up_kernel(x_ref, wg_ref, wu_ref, side_ref, o_ref, side_o_ref):
    x = x_ref[...]
    g = jnp.dot(x, wg_ref[...].astype(BF16), preferred_element_type=F32)
    u = jnp.dot(x, wu_ref[...].astype(BF16), preferred_element_type=F32)
    o_ref[...] = (g * _sigmoid(g) * u).astype(BF16)
    side_o_ref[...] = side_ref[...].astype(BF16)


FFN_UP_BN = 256


def _ffn_up(u, w_up, w_down, lead):
    rows, k = u.shape
    bm, bn = rows // 4, FFN_UP_BN
    side_in, side_out, side_shape, side_bytes = _side_cast_specs((w_down, lead), D_FF // bn)
    vmem = (bm * k * 2 + 4 * k * bn * 4 + 2 * k * bn * 2 + 2 * bm * bn * 2 + 3 * bm * bn * 4 + side_bytes
            + (4 << 20))
    return pl.pallas_call(
        _up_kernel,
        grid=(rows // bm, D_FF // bn),
        in_specs=[pl.BlockSpec((bm, k), lambda i, j: (i, 0), pipeline_mode=pl.Buffered(1)),
                  _w_spec(lead, k, bn, 0),
                  _w_spec(lead, k, bn, D_FF // bn),
                  side_in],
        out_specs=[pl.BlockSpec((bm, bn), lambda i, j: (i, j)), side_out],
        out_shape=[jax.ShapeDtypeStruct((rows, D_FF), BF16), side_shape],
        compiler_params=_params(("parallel", "parallel"), vmem),
        name="ffn_up_swiglu",
    )(u, w_up, w_up, w_down)


def _mm_kernel(*refs, mode, has_side):
    refs = list(refs)
    if has_side:
        side_o_ref = refs.pop()
    o_ref = refs.pop()
    if has_side:
        side_o_ref[...] = refs.pop()[...].astype(BF16)
    x_ref, w_ref, *rest = refs
    acc = jnp.dot(x_ref[...], w_ref[...].astype(BF16), preferred_element_type=F32)
    if mode == "softplus_bias":
        (b_ref,) = rest
        acc = _softplus(acc + b_ref[...])
    elif mode == "gate":
        (g_ref,) = rest
        acc = _sigmoid(g_ref[...].astype(F32)) * acc
    elif mode == "gate_add":
        g_ref, p_ref = rest
        acc = p_ref[...] + _sigmoid(g_ref[...].astype(F32)) * acc
    o_ref[...] = acc.astype(o_ref.dtype)


def _matmul(x, w, *, rows, bm, bn, out_dtype, gate=None, prev=None, softplus_bias=None, single_buffer_x=False,
            w_lead=(), w_col0=0, n=None, side=None, name="matmul"):
    k = x.shape[1]
    n = w.shape[-1] if n is None else n
    mode = "plain" if gate is None else ("gate" if prev is None else "gate_add")
    x_kwargs = dict(pipeline_mode=pl.Buffered(1)) if single_buffer_x else {}
    in_specs = [pl.BlockSpec((bm, k), lambda i, j: (i, 0), **x_kwargs),
                _w_spec(w_lead, k, bn, w_col0 // bn)]
    args = [x, w]
    if softplus_bias is not None:
        mode = "softplus_bias"
        in_specs.append(pl.BlockSpec((1, bn), lambda i, j: (0, j)))
        args.append(softplus_bias)
    if gate is not None:
        g_arr, g_blk0 = gate
        in_specs.append(pl.BlockSpec((bm, bn), lambda i, j: (i, g_blk0 + j)))
        args.append(g_arr)
    if prev is not None:
        in_specs.append(pl.BlockSpec((bm, bn), lambda i, j: (i, j)))
        args.append(prev)
    out_specs = [pl.BlockSpec((bm, bn), lambda i, j: (i, j))]
    out_shape = [jax.ShapeDtypeStruct((rows, n), out_dtype)]
    xbuf = 1 if single_buffer_x else 2
    w_bytes = 2 * k * bn * w.dtype.itemsize + (k * bn * 2 if w.dtype != BF16 else 0)
    vmem = xbuf * bm * k * 2 + w_bytes + 5 * bm * bn * 4 + (4 << 20)
    if side is not None:
        side_in, side_out, side_shape, side_bytes = _side_cast_specs(side, n // bn)
        in_specs.append(side_in)
        args.append(side[0])
        out_specs.append(side_out)
        out_shape.append(side_shape)
        vmem += side_bytes
    out = pl.pallas_call(
        functools.partial(_mm_kernel, mode=mode, has_side=side is not None),
        grid=(rows // bm, n // bn),
        in_specs=in_specs,
        out_specs=out_specs,
        out_shape=out_shape,
        compiler_params=_params(("parallel", "parallel"), vmem),
        name=name,
    )(*args)
    return out if side is not None else out[0]


def _ffn(u, w_up, w_down, lead):
    act, wd = _ffn_up(u, w_up, w_down, lead)
    return _matmul(act, wd, rows=act.shape[0], bm=1024, bn=512, out_dtype=F32, single_buffer_x=True,
                   name="ffn_down")


CONV_BM = CTX_LEN
CONV_HALO = 16
CONV_CB = 512


def _conv_rows_kernel(x_ref, prev_ref, next_ref, w_ref, b_ref, o_ref, *, row_blk0, silu):
    i = row_blk0 + pl.program_id(0)
    lat_blocks = N_LAT // CONV_BM
    per_seq = SEQ // CONV_BM
    is_ctx = i >= lat_blocks
    starts = jnp.logical_or(is_ctx, i % per_seq == 0)
    ends = jnp.logical_or(is_ctx, i % per_seq == per_seq - 1)
    x = x_ref[...].astype(F32)
    before = jnp.where(starts, 0.0, prev_ref[...].astype(F32)[CONV_HALO - 8:])
    after = jnp.where(ends, 0.0, next_ref[...].astype(F32)[:8])
    xe = jnp.concatenate([before, x, after], axis=0)
    t = x.shape[0]
    w = w_ref[...]
    acc = x * w[2:3] + b_ref[...]
    acc += xe[6:6 + t] * w[0:1]
    acc += xe[7:7 + t] * w[1:2]
    acc += xe[9:9 + t] * w[3:4]
    if silu:
        acc = acc * _sigmoid(acc)
    o_ref[...] = acc.astype(o_ref.dtype)


def _conv_rows(src, w, b, *, row_blk0, n_blk, silu, out_dtype):
    width = src.shape[1]
    halo_per_blk = CONV_BM // CONV_HALO
    last_halo = src.shape[0] // CONV_HALO - 1
    return pl.pallas_call(
        functools.partial(_conv_rows_kernel, row_blk0=row_blk0, silu=silu),
        grid=(n_blk, width // CONV_CB),
        in_specs=[pl.BlockSpec((CONV_BM, CONV_CB), lambda i, j: (row_blk0 + i, j)),
                  pl.BlockSpec((CONV_HALO, CONV_CB),
                               lambda i, j: (jnp.maximum((row_blk0 + i) * halo_per_blk - 1, 0), j)),
                  pl.BlockSpec((CONV_HALO, CONV_CB),
                               lambda i, j: (jnp.minimum((row_blk0 + i + 1) * halo_per_blk, last_halo), j)),
                  pl.BlockSpec((4, CONV_CB), lambda i, j: (0, j)),
                  pl.BlockSpec((1, CONV_CB), lambda i, j: (0, j))],
        out_specs=pl.BlockSpec((CONV_BM, CONV_CB), lambda i, j: (i, j)),
        out_shape=jax.ShapeDtypeStruct((n_blk * CONV_BM, width), out_dtype),
        compiler_params=_params(("parallel", "parallel"), 32 << 20),
        name="conv_rows",
    )(src, src, src, w, b)


def _split3(v):
    hi = v.astype(BF16)
    r1 = v - hi.astype(F32)
    mid = r1.astype(BF16)
    lo = (r1 - mid.astype(F32)).astype(BF16)
    return hi, mid, lo


def _dot_exact_rhs(lhs_bf16, v, terms=3):
    out = None
    for piece in _split3(v)[:terms]:
        d = jnp.dot(lhs_bf16, piece, preferred_element_type=F32)
        out = d if out is None else out + d
    return out


def _ssd_kernel(x_ref, b_ref, c_ref, dt_ref, alog_ref, y_ref, state_ref, *, rev):
    L = SSD_CHUNK
    E = SSD_HEADS_PER_GROUP
    P = SSD_HEAD_DIM
    GW = SSD_GROUP_W
    off = E if rev else 0
    s = pl.program_id(2)

    @pl.when(s == 0)
    def _():
        state_ref[...] = jnp.zeros_like(state_ref)

    ii = lax.broadcasted_iota(jnp.int32, (L, L), 0)
    jj = lax.broadcasted_iota(jnp.int32, (L, L), 1)
    tri = (jj >= ii) if rev else (jj <= ii)
    tri_b = jnp.where(tri, 1.0, 0.0).astype(BF16)
    first_half = lax.broadcasted_iota(jnp.int32, (L, 2 * P), 1) < P

    groups = []
    for gi in range(SSD_GROUPS_PER_STEP):
        tile = slice(gi * 128, (gi + 1) * 128)
        dt = dt_ref[:, tile]
        da = dt * (-jnp.exp(alog_ref[:, tile]))
        cs = _dot_exact_rhs(tri_b, da)
        tot = cs[0:1] if rev else cs[L - 1:L]
        log_end = jnp.log(dt) + (tot - cs)
        groups.append((dt, cs, tot, log_end, b_ref[:, tile], c_ref[:, tile]))

    @pl.when(s >= 2)
    def _():
        for gi, (dt, cs, tot, log_end, bm, cm) in enumerate(groups):
            cs_t = cs.T
            dt_t = dt.T
            cb = lax.dot_general(cm, bm, (((1,), (1,)), ((), ())), preferred_element_type=F32)
            for pair in range(E // 2):
                sl = slice(gi * GW + pair * 2 * P, gi * GW + (pair + 1) * 2 * P)
                xp = x_ref[:, sl]
                halves, cols = [], []
                for e in (off + 2 * pair, off + 2 * pair + 1):
                    col = jnp.broadcast_to(cs[:, e:e + 1], (L, L))
                    cols.append(col)
                    lmat = jnp.exp(jnp.where(tri, col - cs_t[e:e + 1, :], -1e30))
                    m = (cb * lmat * dt_t[e:e + 1, :]).astype(BF16)
                    halves.append(jnp.dot(m, xp, preferred_element_type=F32))
                from_start = jnp.exp(jnp.where(first_half, cols[0], cols[1]))
                y_off = jnp.dot(cm, state_ref[:, sl].astype(BF16), preferred_element_type=F32) * from_start
                y_ref[:, sl] = (jnp.where(first_half, halves[0], halves[1]) + y_off).astype(y_ref.dtype)

    for gi, (dt, cs, tot, log_end, bm, cm) in enumerate(groups):
        for pair in range(E // 2):
            sl = slice(gi * GW + pair * 2 * P, gi * GW + (pair + 1) * 2 * P)
            e0 = off + 2 * pair
            w_end = jnp.exp(jnp.where(first_half, jnp.broadcast_to(log_end[:, e0:e0 + 1], (L, 2 * P)),
                                      jnp.broadcast_to(log_end[:, e0 + 1:e0 + 2], (L, 2 * P))))
            x_end = (x_ref[:, sl].astype(F32) * w_end).astype(BF16)
            upd = lax.dot_general(bm, x_end, (((0,), (0,)), ((), ())), preferred_element_type=F32)
            tot_pair = jnp.where(first_half[0:1], jnp.broadcast_to(tot[:, e0:e0 + 1], (1, 2 * P)),
                                 jnp.broadcast_to(tot[:, e0 + 1:e0 + 2], (1, 2 * P)))
            state_ref[:, sl] = state_ref[:, sl] * jnp.exp(tot_pair) + upd


def _ssd_row_block(b, s, rev):
    n_lat = SEQ // SSD_CHUNK
    if rev:
        ctx = N_LAT // SSD_CHUNK + 2 * b + (1 - s)
        lat = n_lat * b + (n_lat + 1 - s)
    else:
        ctx = N_LAT // SSD_CHUNK + 2 * b + s
        lat = n_lat * b + (s - 2)
    return jnp.where(s < 2, ctx, lat)


def _ssd_scan(xconv, dt, alog_p, rev):
    n_lat = SEQ // SSD_CHUNK
    gps = SSD_GROUPS_PER_STEP
    rb = functools.partial(_ssd_row_block, rev=rev)

    def out_block(b, s):
        return _ssd_row_block(b, jnp.maximum(s, 2), rev)

    bw = gps * SSD_D_STATE
    b0 = SSD_D_INNER // bw
    c0 = (SSD_D_INNER + SSD_GN) // bw
    return pl.pallas_call(
        functools.partial(_ssd_kernel, rev=rev),
        grid=(BATCH, SSD_N_GROUPS // gps, n_lat + 2),
        in_specs=[pl.BlockSpec((SSD_CHUNK, gps * SSD_GROUP_W), lambda b, g, s: (rb(b, s), g)),
                  pl.BlockSpec((SSD_CHUNK, bw), lambda b, g, s: (rb(b, s), b0 + g)),
                  pl.BlockSpec((SSD_CHUNK, bw), lambda b, g, s: (rb(b, s), c0 + g)),
                  pl.BlockSpec((SSD_CHUNK, gps * 128), lambda b, g, s: (rb(b, s), g)),
                  pl.BlockSpec((1, gps * 128), lambda b, g, s: (0, g))],
        out_specs=pl.BlockSpec((SSD_CHUNK, gps * SSD_GROUP_W), lambda b, g, s: (out_block(b, s), g)),
        out_shape=jax.ShapeDtypeStruct((N_LAT, SSD_D_INNER), BF16),
        scratch_shapes=[pltpu.VMEM((SSD_D_STATE, gps * SSD_GROUP_W), F32)],
        compiler_params=_params(("parallel", "parallel", "arbitrary"), 32 << 20),
        name="ssd_scan_bwd" if rev else "ssd_scan_fwd",
    )(xconv, xconv, xconv, dt, alog_p)


def _ssd_gate_kernel(yf_ref, yb_ref, xs_ref, z_ref, d_ref, g_ref, o_ref):
    z = z_ref[...].astype(F32)
    y = yf_ref[...].astype(F32) + yb_ref[...].astype(F32) + d_ref[...] * xs_ref[...].astype(F32)
    y = y * (z * _sigmoid(z))
    o_ref[...] = (_rms(y) * g_ref[...]).astype(BF16)


def _ssd_gate(yf, yb, xconv, proj, d_exp, norm_g, bm=512):
    blk = lambda: pl.BlockSpec((bm, SSD_GROUP_W), lambda i, g: (i, g))
    vec = lambda: pl.BlockSpec((1, SSD_GROUP_W), lambda i, g: (0, g))
    return pl.pallas_call(
        _ssd_gate_kernel,
        grid=(N_LAT // bm, SSD_N_GROUPS),
        in_specs=[blk(), blk(), blk(), blk(), vec(), vec()],
        out_specs=blk(),
        out_shape=jax.ShapeDtypeStruct((N_LAT, SSD_D_INNER), BF16),
        compiler_params=_params(("parallel", "parallel"), 32 << 20),
        name="ssd_gate_norm",
    )(yf, yb, xconv, proj, d_exp, norm_g)


def _rg_conv_lat_kernel(x_ref, w_ref, b_ref, o_ref):
    w = w_ref[...]
    bias = b_ref[...]
    col = lax.broadcasted_iota(jnp.int32, (GRID_W, x_ref.shape[2]), 0)

    def row(r):
        return x_ref[r].astype(F32)

    def prev_col(v):
        return jnp.where(col >= 1, pltpu.roll(v, 1, axis=0), 0.0)

    def next_col(v):
        return jnp.where(col < GRID_W - 1, pltpu.roll(v, GRID_W - 1, axis=0), 0.0)

    for r in range(GRID_H):
        m2 = row(r - 2) if r >= 2 else prev_col(row(r - 2 + GRID_H))
        m1 = row(r - 1) if r >= 1 else prev_col(row(GRID_H - 1))
        p1 = row(r + 1) if r < GRID_H - 1 else next_col(row(0))
        o_ref[r] = m2 * w[0:1] + m1 * w[1:2] + row(r) * w[2:3] + p1 * w[3:4] + bias


def _rg_conv_lat(xr3, w, b):
    return pl.pallas_call(
        _rg_conv_lat_kernel,
        grid=(BATCH, RG_WIDTH_P // CONV_CB),
        in_specs=[pl.BlockSpec((GRID_H, GRID_W, CONV_CB), lambda bb, j: (bb, 0, j)),
                  pl.BlockSpec((4, CONV_CB), lambda bb, j: (0, j)),
                  pl.BlockSpec((1, CONV_CB), lambda bb, j: (0, j))],
        out_specs=pl.BlockSpec((GRID_H, GRID_W, CONV_CB), lambda bb, j: (bb, 0, j)),
        out_shape=jax.ShapeDtypeStruct((BATCH * GRID_H, GRID_W, RG_WIDTH_P), F32),
        compiler_params=_params(("parallel", "parallel"), 32 << 20),
        name="rg_conv_lat",
    )(xr3, w, b)


def _rg_scan_kernel(xl_ref, xc_ref, wa_ref, wx_ref, ba_ref, bx_ref, lam_ref, h_ref, carry_ref, a_ref, b_ref, *, rev):
    s = pl.program_id(2)

    @pl.when(s == 0)
    def _():
        carry_ref[...] = jnp.zeros_like(carry_ref)

    x3 = jnp.where(s == 0, xc_ref[...], xl_ref[...])
    x2 = x3.reshape(GRID_H * RG_SEG, RG_CB)
    xb = x2.astype(BF16)
    neg_c_sp = -RG_C * _softplus(-lam_ref[...])
    for k in range(RG_CB // RG_BLOCK_P):
        sl = slice(k * RG_BLOCK_P, (k + 1) * RG_BLOCK_P)
        xk = xb[:, sl]
        r = _sigmoid(jnp.dot(xk, wa_ref[k], preferred_element_type=F32) + ba_ref[:, sl])
        i = _sigmoid(jnp.dot(xk, wx_ref[k], preferred_element_type=F32) + bx_ref[:, sl])
        log_a = neg_c_sp[:, sl] * r
        a_ref[:, :, sl] = jnp.exp(log_a).reshape(GRID_H, RG_SEG, RG_BLOCK_P)
        b_ref[:, :, sl] = (jnp.sqrt(_neg_expm1_2x(log_a)) * (i * x2[:, sl])).reshape(GRID_H, RG_SEG, RG_BLOCK_P)

    order = range(GRID_H - 1, -1, -1) if rev else range(GRID_H)
    p = None
    for r in order:
        a = a_ref[r]
        if p is None:
            p, hloc = a, b_ref[r]
        else:
            p, hloc = a * p, a * hloc + b_ref[r]
        a_ref[r] = p
        b_ref[r] = hloc
    seg_row = lax.broadcasted_iota(jnp.int32, (RG_SEG, RG_CB), 0)
    state = carry_ref[...]
    enter = jnp.zeros((RG_SEG, RG_CB), F32)
    for c in (range(RG_SEG - 1, -1, -1) if rev else range(RG_SEG)):
        enter = jnp.where(seg_row == c, state, enter)
        state = hloc[c:c + 1] + p[c:c + 1] * state
    carry_ref[...] = state

    @pl.when(s >= 1)
    def _():
        for r in range(GRID_H):
            h_ref[r] = b_ref[r] + a_ref[r] * enter


def _rg_scan(xl3, xc3, wa, wx, ba, bx, lam, rev):
    n_segblk = GRID_W // RG_SEG

    def seg_block(s):
        return jnp.clip((n_segblk - s) if rev else (s - 1), 0, n_segblk - 1)

    blk = (GRID_H, RG_SEG, RG_CB)
    nk = RG_CB // RG_BLOCK_P
    wspec = lambda: pl.BlockSpec((nk, RG_BLOCK_P, RG_BLOCK_P), lambda cb, b, s: (cb, 0, 0))
    vspec = lambda: pl.BlockSpec((1, RG_CB), lambda cb, b, s: (0, cb))
    return pl.pallas_call(
        functools.partial(_rg_scan_kernel, rev=rev),
        grid=(RG_WIDTH_P // RG_CB, BATCH, n_segblk + 1),
        in_specs=[pl.BlockSpec(blk, lambda cb, b, s: (b, seg_block(s), cb)),
                  pl.BlockSpec(blk, lambda cb, b, s: (b, 0, cb)),
                  wspec(), wspec(), vspec(), vspec(), vspec()],
        out_specs=pl.BlockSpec(blk, lambda cb, b, s: (b, seg_block(s), cb)),
        out_shape=jax.ShapeDtypeStruct((BATCH * GRID_H, GRID_W, RG_WIDTH_P), F32),
        scratch_shapes=[pltpu.VMEM((1, RG_CB), F32), pltpu.VMEM(blk, F32), pltpu.VMEM(blk, F32)],
        compiler_params=_params(("parallel", "parallel", "arbitrary"), 40 << 20),
        name="rg_scan_bwd" if rev else "rg_scan_fwd",
    )(xl3, xc3, wa, wx, ba, bx, lam)


def _rg_gate_kernel(g_ref, hf_ref, hb_ref, o_ref):
    g = g_ref[...].astype(F32)
    o_ref[...] = (jax.nn.gelu(g) * (hf_ref[...] + hb_ref[...])).astype(BF16)


def _rg_gate(gg, hf, hb, bm=512, bn=1536):
    return pl.pallas_call(
        _rg_gate_kernel,
        grid=(N_LAT // bm, RG_WIDTH_P // bn),
        in_specs=[pl.BlockSpec((bm, bn), lambda i, j: (i, j)),
                  pl.BlockSpec((bm, bn), lambda i, j: (i, j)),
                  pl.BlockSpec((bm, bn), lambda i, j: (i, j))],
        out_specs=pl.BlockSpec((bm, bn), lambda i, j: (i, j)),
        out_shape=jax.ShapeDtypeStruct((N_LAT, RG_WIDTH_P), BF16),
        compiler_params=_params(("parallel", "parallel"), 32 << 20),
        name="rg_gate",
    )(gg, hf, hb)


def _pad_cols_kernel(w_ref, o_ref):
    w = w_ref[...]
    zeros = jnp.zeros((w.shape[0], RG_BLOCK_P - RG_BLOCK), BF16)
    for k in range(RG_N_BLOCKS):
        blk = w[:, k * RG_BLOCK:(k + 1) * RG_BLOCK].astype(BF16)
        o_ref[:, k * RG_BLOCK_P:(k + 1) * RG_BLOCK_P] = jnp.concatenate([blk, zeros], axis=1)


def _pad_cols(w, col0, bm=512):
    rows = w.shape[0]
    return pl.pallas_call(
        _pad_cols_kernel,
        grid=(rows // bm,),
        in_specs=[pl.BlockSpec((pl.Element(bm), pl.Element(RG_WIDTH)), lambda i: (i * bm, col0))],
        out_specs=pl.BlockSpec((bm, RG_WIDTH_P), lambda i: (i, 0)),
        out_shape=jax.ShapeDtypeStruct((rows, RG_WIDTH_P), BF16),
        compiler_params=_params(("parallel",), 48 << 20),
        name="pad_rg_cols",
    )(w)


def _pad_rg(a, axis):
    axis = axis % a.ndim
    shape = a.shape[:axis] + (RG_N_BLOCKS, RG_BLOCK) + a.shape[axis + 1:]
    pad = [(0, 0)] * (a.ndim + 1)
    pad[axis + 1] = (0, RG_BLOCK_P - RG_BLOCK)
    out = jnp.pad(a.reshape(shape), pad)
    return out.reshape(a.shape[:axis] + (RG_WIDTH_P,) + a.shape[axis + 1:])


def _group_lanes(a):
    lead = a.shape[:-1]
    e = SSD_HEADS_PER_GROUP
    t = a.reshape(lead + (2, SSD_N_GROUPS, e))
    t = jnp.moveaxis(t, -3, -2).reshape(lead + (SSD_N_GROUPS, 2 * e))
    t = jnp.pad(t, [(0, 0)] * len(lead) + [(0, 0), (0, 128 - 2 * e)])
    return t.reshape(lead + (DT_W,))


def kernel(x, c, ctx, c_ctx, w_ada, b_ada, norm_g, ffn_w_up, ffn_w_down, w_in, ssd_conv_w, ssd_conv_b, ssd_dt_bias,
           ssd_a_log, ssd_d, ssd_norm_g, w_ssd_out, rg_conv_w, rg_conv_b, rg_w_a, rg_b_a, rg_w_x, rg_b_x, rg_lam,
           w_rg_out, w_out):
    l = 0
    g = norm_g[l]

    wi = w_in[l]
    w_dt = _group_lanes(wi[:, _S2:_S3]).astype(BF16)
    w_gg = _pad_cols(wi, _S3)
    w_xr = _pad_cols(wi, _S4)
    dt_bias_p = _group_lanes(ssd_dt_bias[l]).reshape(1, DT_W)
    alog_p = _group_lanes(ssd_a_log[l].reshape(2 * SSD_N_HEADS)).reshape(1, DT_W)
    d_exp = jnp.repeat(ssd_d[l], SSD_HEAD_DIM).reshape(1, SSD_D_INNER)
    ssd_ng = ssd_norm_g[l].reshape(1, SSD_D_INNER)
    ssd_cw, ssd_cb = ssd_conv_w[l], ssd_conv_b[l].reshape(1, SSD_CONV_DIM)
    rg_cw, rg_cb = _pad_rg(rg_conv_w[l], 1), _pad_rg(rg_conv_b[l], 0).reshape(1, RG_WIDTH_P)
    pad_w = lambda w: jnp.pad(w, ((0, 0), (0, RG_BLOCK_P - RG_BLOCK), (0, RG_BLOCK_P - RG_BLOCK))).astype(BF16)
    rg_wa = [pad_w(rg_w_a[l, d]) for d in range(2)]
    rg_wx = [pad_w(rg_w_x[l, d]) for d in range(2)]
    rg_ba = [_pad_rg(rg_b_a[l, d], 0).reshape(1, RG_WIDTH_P) for d in range(2)]
    rg_bx = [_pad_rg(rg_b_x[l, d], 0).reshape(1, RG_WIDTH_P) for d in range(2)]
    rg_lm = [_pad_rg(rg_lam[l, d], 0).reshape(1, RG_WIDTH_P) for d in range(2)]
    w_rg_o = _pad_rg(w_rg_out[l].astype(BF16), 0)
    w_o = w_out[l].astype(BF16)

    c8 = jnp.concatenate([c, c_ctx[None], jnp.zeros((8 - BATCH - 1, D_MODEL), F32)], axis=0)
    mod3 = _ada(c8, w_ada[l], b_ada[l].reshape(1, -1)).reshape(8, 1, N_MOD * D_MODEL)

    h0 = (x.reshape(N_LAT, D_MODEL), ctx.reshape(N_CTX, D_MODEL))

    u1 = _normmod(h0, g[0], mod3, 0, N_TOK)
    y1 = _ffn(u1, ffn_w_up, ffn_w_down, (l, 0))
    h1, u2 = _residual(h0, y1, g[1], mod3, 0, MACARON_W, nxt=(g[2], 1))

    direct = functools.partial(_matmul, u2, w_in, w_lead=(l,), out_dtype=BF16, single_buffer_x=True)
    z = direct(rows=N_LAT, bm=N_LAT // 4, bn=512, w_col0=0, n=_S1, name="in_proj_z")
    xbc = direct(rows=N_TOK, bm=N_TOK // 4, bn=512, w_col0=_S1, n=SSD_CONV_DIM, name="in_proj_xbc")
    mg, w_ssd_o = direct(rows=N_LAT, bm=N_LAT // 4, bn=256, w_col0=_S5, n=2 * D_MODEL, side=(w_ssd_out, (l,)),
                         name="in_proj_merge_gates")
    proj = functools.partial(_matmul, u2, bm=1024, bn=1024)
    gg = proj(w_gg, rows=N_LAT, out_dtype=BF16, name="in_proj_gelu_gate")
    xr = proj(w_xr, rows=N_TOK, out_dtype=BF16, name="in_proj_rg")
    dt = proj(w_dt, rows=N_TOK, out_dtype=F32, softplus_bias=dt_bias_p, name="in_proj_dt")

    xconv = _conv_rows(xbc, ssd_cw, ssd_cb, row_blk0=0, n_blk=N_TOK // CONV_BM, silu=True, out_dtype=BF16)
    y_f = _ssd_scan(xconv, dt, alog_p, rev=False)
    y_b = _ssd_scan(xconv, dt, alog_p, rev=True)
    y_n = _ssd_gate(y_f, y_b, xconv, z, d_exp, ssd_ng)

    xr_lat = _rg_conv_lat(xr.reshape(N_TOK // GRID_W, GRID_W, RG_WIDTH_P), rg_cw, rg_cb)
    xr_ctx = _conv_rows(xr, rg_cw, rg_cb, row_blk0=N_LAT // CONV_BM, n_blk=N_CTX // CONV_BM, silu=False,
                        out_dtype=F32)
    xr_ctx = xr_ctx.reshape(BATCH, RG_SEG, GRID_H, RG_WIDTH_P).transpose(0, 2, 1, 3)
    xr_ctx = xr_ctx.reshape(BATCH * GRID_H, RG_SEG, RG_WIDTH_P)
    h_f = _rg_scan(xr_lat, xr_ctx, rg_wa[0], rg_wx[0], rg_ba[0], rg_bx[0], rg_lm[0], rev=False)
    h_b = _rg_scan(xr_lat, xr_ctx, rg_wa[1], rg_wx[1], rg_ba[1], rg_bx[1], rg_lm[1], rev=True)
    r_in = _rg_gate(gg, h_f.reshape(N_LAT, RG_WIDTH_P), h_b.reshape(N_LAT, RG_WIDTH_P))

    m1 = _matmul(y_n, w_ssd_o, rows=N_LAT, bm=1024, bn=512, out_dtype=F32, gate=(mg, 0), single_buffer_x=True,
                 name="ssd_out_proj")
    m2 = _matmul(r_in, w_rg_o, rows=N_LAT, bm=1024, bn=512, out_dtype=BF16, gate=(mg, D_MODEL // 512), prev=m1,
                 name="rg_out_proj")
    m3 = _matmul(m2, w_o, rows=N_LAT, bm=1024, bn=512, out_dtype=F32, name="out_proj")
    h2, u3 = _residual(h1, m3, g[3], mod3, 1, 1.0, nxt=(g[4], 2))

    y3 = _ffn(u3, ffn_w_up, ffn_w_down, (l, 1))
    out = _residual(h2, y3, g[5], mod3, 2, MACARON_W)
    return out.reshape(BATCH, SEQ, D_MODEL)
```

```python
import functools
import math

import jax
import jax.numpy as jnp
from jax import lax
from jax.experimental import pallas as pl
from jax.experimental.pallas import tpu as pltpu

F32 = jnp.float32
BF16 = jnp.bfloat16

D_MODEL = 4096
BATCH = 4
SEQ = 2048
GRID_W = 64
GRID_H = SEQ // GRID_W
CTX_LEN = 256
N_MOD = 9
EPS = 1e-6
MACARON_W = 0.5

N_LAT = BATCH * SEQ
N_CTX = BATCH * CTX_LEN
N_TOK = N_LAT + N_CTX

SSD_D_INNER = 2 * D_MODEL
SSD_HEAD_DIM = 64
SSD_N_HEADS = SSD_D_INNER // SSD_HEAD_DIM
SSD_N_GROUPS = 8
SSD_HEADS_PER_GROUP = SSD_N_HEADS // SSD_N_GROUPS
SSD_GROUP_W = SSD_D_INNER // SSD_N_GROUPS
SSD_D_STATE = 128
SSD_GN = SSD_N_GROUPS * SSD_D_STATE
SSD_CONV_DIM = SSD_D_INNER + 2 * SSD_GN
SSD_CHUNK = 128
SSD_GROUPS_PER_STEP = 4

RG_WIDTH = 5376
RG_N_BLOCKS = 16
RG_BLOCK = RG_WIDTH // RG_N_BLOCKS
RG_BLOCK_P = 384
RG_WIDTH_P = RG_N_BLOCKS * RG_BLOCK_P
RG_C = 8.0
RG_SEG = 8
RG_CB = 4 * RG_BLOCK_P

D_FF = 11008

_S1 = SSD_D_INNER
_S2 = _S1 + SSD_CONV_DIM
_S3 = _S2 + 2 * SSD_N_HEADS
_S4 = _S3 + RG_WIDTH
_S5 = _S4 + RG_WIDTH

DT_W = SSD_N_GROUPS * 128

VMEM_CAP = 56 * 1024 * 1024


def _params(sem, vmem_bytes):
    return pltpu.CompilerParams(dimension_semantics=sem,
                                vmem_limit_bytes=int(min(VMEM_CAP, max(32 << 20, vmem_bytes))))


def _softplus(x):
    return jnp.maximum(x, 0.0) + jnp.log1p(jnp.exp(-jnp.abs(x)))


def _sigmoid(x):
    return 0.5 * jnp.tanh(0.5 * x) + 0.5


def _neg_expm1_2x(x):
    t = jnp.tanh(x)
    return (t + t) / (t - 1.0)


ADA_BN = 512


def _ada_kernel(c_ref, w_ref, b_ref, o_ref):
    c = c_ref[...]
    sc = (c * _sigmoid(c)).astype(BF16)
    o_ref[...] = jnp.dot(sc, w_ref[...].astype(BF16), preferred_element_type=F32) + b_ref[...]


def _ada(c8, w_ada, b_ada):
    n = w_ada.shape[1]
    return pl.pallas_call(
        _ada_kernel,
        grid=(n // ADA_BN,),
        in_specs=[pl.BlockSpec((8, D_MODEL), lambda j: (0, 0)),
                  pl.BlockSpec((D_MODEL, ADA_BN), lambda j: (0, j)),
                  pl.BlockSpec((1, ADA_BN), lambda j: (0, j))],
        out_specs=pl.BlockSpec((8, ADA_BN), lambda j: (0, j)),
        out_shape=jax.ShapeDtypeStruct((8, n), F32),
        compiler_params=_params(("parallel",), 2 * D_MODEL * ADA_BN * 4 + D_MODEL * ADA_BN * 4 + (4 << 20)),
        name="ada_mod",
    )(c8, w_ada, b_ada)


ROW_BM = 256


def _mod_row(i):
    return jnp.where(i < N_LAT // ROW_BM, i // (SEQ // ROW_BM), BATCH)


def _mod_spec(slot):
    return pl.BlockSpec((None, 1, D_MODEL), lambda i: (_mod_row(i), 0, slot))


def _vec_spec():
    return pl.BlockSpec((1, D_MODEL), lambda i: (0, 0))


def _row_spec():
    return pl.BlockSpec((ROW_BM, D_MODEL), lambda i: (i, 0))


N_LAT_BLK = N_LAT // ROW_BM


def _token_specs(h):
    if not isinstance(h, tuple):
        return [_row_spec()], [h]
    return [pl.BlockSpec((ROW_BM, D_MODEL), lambda i: (jnp.minimum(i, N_LAT_BLK - 1), 0)),
            pl.BlockSpec((ROW_BM, D_MODEL), lambda i: (jnp.maximum(i - N_LAT_BLK, 0), 0))], list(h)


def _load_tokens(refs):
    if len(refs) == 1:
        return refs[0][...]
    return jnp.where(pl.program_id(0) < N_LAT_BLK, refs[0][...], refs[1][...])


def _rms(x):
    return x * lax.rsqrt(jnp.mean(x * x, axis=-1, keepdims=True) + EPS)


def _normmod_kernel(*refs, n_h):
    g_ref, sc_ref, sh_ref, u_ref = refs[n_h:]
    u = _rms(_load_tokens(refs[:n_h])) * g_ref[...]
    u_ref[...] = (u * (1.0 + sc_ref[...]) + sh_ref[...]).astype(BF16)


def _normmod(h, g, mod3, slot, rows):
    h_specs, h_args = _token_specs(h)
    return pl.pallas_call(
        functools.partial(_normmod_kernel, n_h=len(h_args)),
        grid=(rows // ROW_BM,),
        in_specs=h_specs + [_vec_spec(), _mod_spec(3 * slot + 1), _mod_spec(3 * slot)],
        out_specs=_row_spec(),
        out_shape=jax.ShapeDtypeStruct((rows, D_MODEL), BF16),
        compiler_params=_params(("parallel",), 40 << 20),
        name="normmod",
    )(*h_args, g.reshape(1, D_MODEL), mod3, mod3)


def _res_kernel(*refs, n_h, coef, with_next):
    y_ref, gpost_ref, gate_ref = refs[n_h:n_h + 3]
    rest = refs[n_h + 3:]
    h2 = _load_tokens(refs[:n_h]) + (coef * gate_ref[...]) * (_rms(y_ref[...]) * gpost_ref[...])
    if with_next:
        gn_ref, sc_ref, sh_ref, ho_ref, u_ref = rest
        ho_ref[...] = h2
        u = _rms(h2) * gn_ref[...]
        u_ref[...] = (u * (1.0 + sc_ref[...]) + sh_ref[...]).astype(BF16)
    else:
        (ho_ref,) = rest
        ho_ref[...] = h2


def _residual(h, y, g_post, mod3, slot, coef, nxt=None):
    rows = y.shape[0]
    h_specs, h_args = _token_specs(h)
    in_specs = h_specs + [_row_spec(), _vec_spec(), _mod_spec(3 * slot + 2)]
    args = h_args + [y, g_post.reshape(1, D_MODEL), mod3]
    out_specs = [_row_spec()]
    out_shape = [jax.ShapeDtypeStruct((rows, D_MODEL), F32)]
    if nxt is not None:
        g_next, slot_next = nxt
        in_specs += [_vec_spec(), _mod_spec(3 * slot_next + 1), _mod_spec(3 * slot_next)]
        args += [g_next.reshape(1, D_MODEL), mod3, mod3]
        out_specs.append(_row_spec())
        out_shape.append(jax.ShapeDtypeStruct((rows, D_MODEL), BF16))
    out = pl.pallas_call(
        functools.partial(_res_kernel, n_h=len(h_args), coef=coef, with_next=nxt is not None),
        grid=(rows // ROW_BM,),
        in_specs=in_specs, out_specs=out_specs, out_shape=out_shape,
        compiler_params=_params(("parallel",), 48 << 20),
        name="residual_norm",
    )(*args)
    return out if nxt is not None else out[0]


def _w_spec(lead, k, bn, col_blk0):
    squeezed = (None,) * len(lead)
    return pl.BlockSpec(squeezed + (k, bn), lambda i, j: tuple(lead) + (0, col_blk0 + j))


def _side_cast_specs(side, n_col_blocks):
    arr, lead = side
    rows, cols = arr.shape[-2:]
    r = rows // n_col_blocks

    def blk(i, j):
        return jnp.where(i == 0, j, n_col_blocks - 1)

    squeezed = (None,) * len(lead)
    in_spec = pl.BlockSpec(squeezed + (r, cols), lambda i, j: tuple(lead) + (blk(i, j), 0))
    out_spec = pl.BlockSpec((r, cols), lambda i, j: (blk(i, j), 0))
    return in_spec, out_spec, jax.ShapeDtypeStruct((rows, cols), BF16), 2 * r * cols * (4 + 2)


def _side_cast(side_ref, side_o_ref):
    @pl.when(pl.program_id(0) == 0)
    def _():
        side_o_ref[...] = side_ref[...].astype(BF16)


def _up_kernel(x_ref, wg_ref, wu_ref, side_ref, o_ref, side_o_ref):
    x = x_ref[...]
    g = jnp.dot(x, wg_ref[...].astype(BF16), preferred_element_type=F32)
    u = jnp.dot(x, wu_ref[...].astype(BF16), preferred_element_type=F32)
    o_ref[...] = (g * _sigmoid(g) * u).astype(BF16)
    _side_cast(side_ref, side_o_ref)


FFN_UP_BN = 256


def _ffn_up(u, w_up, w_down, lead):
    rows, k = u.shape
    bm, bn = rows // 4, FFN_UP_BN
    side_in, side_out, side_shape, side_bytes = _side_cast_specs((w_down, lead), D_FF // bn)
    vmem = (bm * k * 2 + 4 * k * bn * 4 + 2 * k * bn * 2 + 2 * bm * bn * 2 + 3 * bm * bn * 4 + side_bytes
            + (4 << 20))
    return pl.pallas_call(
        _up_kernel,
        grid=(rows // bm, D_FF // bn),
        in_specs=[pl.BlockSpec((bm, k), lambda i, j: (i, 0), pipeline_mode=pl.Buffered(1)),
                  _w_spec(lead, k, bn, 0),
                  _w_spec(lead, k, bn, D_FF // bn),
                  side_in],
        out_specs=[pl.BlockSpec((bm, bn), lambda i, j: (i, j)), side_out],
        out_shape=[jax.ShapeDtypeStruct((rows, D_FF), BF16), side_shape],
        compiler_params=_params(("arbitrary", "arbitrary"), vmem),
        name="ffn_up_swiglu",
    )(u, w_up, w_up, w_down)


def _mm_kernel(*refs, mode, has_side):
    refs = list(refs)
    if has_side:
        side_o_ref = refs.pop()
    o_ref = refs.pop()
    if has_side:
        _side_cast(refs.pop(), side_o_ref)
    x_ref, w_ref, *rest = refs
    acc = jnp.dot(x_ref[...], w_ref[...].astype(BF16), preferred_element_type=F32)
    if mode == "softplus_bias":
        (b_ref,) = rest
        acc = _softplus(acc + b_ref[...])
    elif mode == "gate":
        (g_ref,) = rest
        acc = _sigmoid(g_ref[...].astype(F32)) * acc
    elif mode == "gate_add":
        g_ref, p_ref = rest
        acc = p_ref[...] + _sigmoid(g_ref[...].astype(F32)) * acc
    o_ref[...] = acc.astype(o_ref.dtype)


def _matmul(x, w, *, rows, bm, bn, out_dtype, gate=None, prev=None, softplus_bias=None, single_buffer_x=False,
            w_lead=(), w_col0=0, n=None, side=None, name="matmul"):
    k = x.shape[1]
    n = w.shape[-1] if n is None else n
    mode = "plain" if gate is None else ("gate" if prev is None else "gate_add")
    x_kwargs = dict(pipeline_mode=pl.Buffered(1)) if single_buffer_x else {}
    in_specs = [pl.BlockSpec((bm, k), lambda i, j: (i, 0), **x_kwargs),
                _w_spec(w_lead, k, bn, w_col0 // bn)]
    args = [x, w]
    if softplus_bias is not None:
        mode = "softplus_bias"
        in_specs.append(pl.BlockSpec((1, bn), lambda i, j: (0, j)))
        args.append(softplus_bias)
    if gate is not None:
        g_arr, g_blk0 = gate
        in_specs.append(pl.BlockSpec((bm, bn), lambda i, j: (i, g_blk0 + j)))
        args.append(g_arr)
    if prev is not None:
        in_specs.append(pl.BlockSpec((bm, bn), lambda i, j: (i, j)))
        args.append(prev)
    out_specs = [pl.BlockSpec((bm, bn), lambda i, j: (i, j))]
    out_shape = [jax.ShapeDtypeStruct((rows, n), out_dtype)]
    xbuf = 1 if single_buffer_x else 2
    w_bytes = 2 * k * bn * w.dtype.itemsize + (k * bn * 2 if w.dtype != BF16 else 0)
    vmem = xbuf * bm * k * 2 + w_bytes + 5 * bm * bn * 4 + (4 << 20)
    if side is not None:
        side_in, side_out, side_shape, side_bytes = _side_cast_specs(side, n // bn)
        in_specs.append(side_in)
        args.append(side[0])
        out_specs.append(side_out)
        out_shape.append(side_shape)
        vmem += side_bytes
    out = pl.pallas_call(
        functools.partial(_mm_kernel, mode=mode, has_side=side is not None),
        grid=(rows // bm, n // bn),
        in_specs=in_specs,
        out_specs=out_specs,
        out_shape=out_shape,
        compiler_params=_params(("arbitrary", "arbitrary") if side is not None else ("parallel", "parallel"), vmem),
        name=name,
    )(*args)
    return out if side is not None else out[0]


def _ffn(u, w_up, w_down, lead):
    act, wd = _ffn_up(u, w_up, w_down, lead)
    return _matmul(act, wd, rows=act.shape[0], bm=1024, bn=512, out_dtype=F32, single_buffer_x=True,
                   name="ffn_down")


CONV_BM = CTX_LEN
CONV_HALO = 16
CONV_CB = 2048


def _conv_rows_kernel(x_ref, prev_ref, next_ref, w_ref, b_ref, o_ref, *, row_blk0, silu):
    i = row_blk0 + pl.program_id(0)
    lat_blocks = N_LAT // CONV_BM
    per_seq = SEQ // CONV_BM
    is_ctx = i >= lat_blocks
    starts = jnp.logical_or(is_ctx, i % per_seq == 0)
    ends = jnp.logical_or(is_ctx, i % per_seq == per_seq - 1)
    x = x_ref[...].astype(F32)
    before = jnp.where(starts, 0.0, prev_ref[...].astype(F32)[CONV_HALO - 8:])
    after = jnp.where(ends, 0.0, next_ref[...].astype(F32)[:8])
    t = x.shape[0]
    w = w_ref[...]
    bias = b_ref[...]

    def taps(v):
        acc = v * w[2:3] + bias
        acc += pltpu.roll(v, 2, axis=0) * w[0:1]
        acc += pltpu.roll(v, 1, axis=0) * w[1:2]
        acc += pltpu.roll(v, v.shape[0] - 1, axis=0) * w[3:4]
        return acc * _sigmoid(acc) if silu else acc

    edge = 16
    top = taps(jnp.concatenate([before, x[:edge + 8]], axis=0))[8:8 + edge]
    bottom = taps(jnp.concatenate([x[t - edge - 8:], after], axis=0))[8:8 + edge]
    o_ref[...] = taps(x).astype(o_ref.dtype)
    o_ref[0:edge] = top.astype(o_ref.dtype)
    o_ref[t - edge:t] = bottom.astype(o_ref.dtype)


def _conv_rows(src, w, b, *, row_blk0, n_blk, silu, out_dtype):
    width = src.shape[1]
    halo_per_blk = CONV_BM // CONV_HALO
    last_halo = src.shape[0] // CONV_HALO - 1
    return pl.pallas_call(
        functools.partial(_conv_rows_kernel, row_blk0=row_blk0, silu=silu),
        grid=(n_blk, width // CONV_CB),
        in_specs=[pl.BlockSpec((CONV_BM, CONV_CB), lambda i, j: (row_blk0 + i, j)),
                  pl.BlockSpec((CONV_HALO, CONV_CB),
                               lambda i, j: (jnp.maximum((row_blk0 + i) * halo_per_blk - 1, 0), j)),
                  pl.BlockSpec((CONV_HALO, CONV_CB),
                               lambda i, j: (jnp.minimum((row_blk0 + i + 1) * halo_per_blk, last_halo), j)),
                  pl.BlockSpec((4, CONV_CB), lambda i, j: (0, j)),
                  pl.BlockSpec((1, CONV_CB), lambda i, j: (0, j))],
        out_specs=pl.BlockSpec((CONV_BM, CONV_CB), lambda i, j: (i, j)),
        out_shape=jax.ShapeDtypeStruct((n_blk * CONV_BM, width), out_dtype),
        compiler_params=_params(("parallel", "parallel"), 32 << 20),
        name="conv_rows",
    )(src, src, src, w, b)


def _split3(v):
    hi = v.astype(BF16)
    r1 = v - hi.astype(F32)
    mid = r1.astype(BF16)
    lo = (r1 - mid.astype(F32)).astype(BF16)
    return hi, mid, lo


def _dot_exact_rhs(lhs_bf16, v, terms=3):
    out = None
    for piece in _split3(v)[:terms]:
        d = jnp.dot(lhs_bf16, piece, preferred_element_type=F32)
        out = d if out is None else out + d
    return out


def _ssd_kernel(x_ref, b_ref, c_ref, dt_ref, alog_ref, y_ref, state_ref, *, rev):
    L = SSD_CHUNK
    E = SSD_HEADS_PER_GROUP
    P = SSD_HEAD_DIM
    GW = SSD_GROUP_W
    off = E if rev else 0
    s = pl.program_id(2)

    @pl.when(s == 0)
    def _():
        state_ref[...] = jnp.zeros_like(state_ref)

    ii = lax.broadcasted_iota(jnp.int32, (L, L), 0)
    jj = lax.broadcasted_iota(jnp.int32, (L, L), 1)
    tri = (jj >= ii) if rev else (jj <= ii)
    tri_b = jnp.where(tri, 1.0, 0.0).astype(BF16)
    first_half = lax.broadcasted_iota(jnp.int32, (L, 2 * P), 1) < P

    groups = []
    for gi in range(SSD_GROUPS_PER_STEP):
        tile = slice(gi * 128, (gi + 1) * 128)
        dt = dt_ref[:, tile]
        da = dt * (-jnp.exp(alog_ref[:, tile]))
        cs = _dot_exact_rhs(tri_b, da)
        tot = cs[0:1] if rev else cs[L - 1:L]
        log_end = jnp.log(dt) + (tot - cs)
        groups.append((dt, cs, tot, log_end, b_ref[:, tile], c_ref[:, tile]))

    @pl.when(s >= 2)
    def _():
        for gi, (dt, cs, tot, log_end, bm, cm) in enumerate(groups):
            cs_t = cs.T
            dt_t = dt.T
            cb = lax.dot_general(cm, bm, (((1,), (1,)), ((), ())), preferred_element_type=F32)
            for pair in range(E // 2):
                sl = slice(gi * GW + pair * 2 * P, gi * GW + (pair + 1) * 2 * P)
                xp = x_ref[:, sl]
                halves, cols = [], []
                for e in (off + 2 * pair, off + 2 * pair + 1):
                    col = jnp.broadcast_to(cs[:, e:e + 1], (L, L))
                    cols.append(col)
                    lmat = jnp.exp(jnp.where(tri, col - cs_t[e:e + 1, :], -1e30))
                    m = (cb * lmat * dt_t[e:e + 1, :]).astype(BF16)
                    halves.append(jnp.dot(m, xp, preferred_element_type=F32))
                from_start = jnp.exp(jnp.where(first_half, cols[0], cols[1]))
                y_off = jnp.dot(cm, state_ref[:, sl].astype(BF16), preferred_element_type=F32) * from_start
                y_ref[:, sl] = (jnp.where(first_half, halves[0], halves[1]) + y_off).astype(y_ref.dtype)

    for gi, (dt, cs, tot, log_end, bm, cm) in enumerate(groups):
        for pair in range(E // 2):
            sl = slice(gi * GW + pair * 2 * P, gi * GW + (pair + 1) * 2 * P)
            e0 = off + 2 * pair
            w_end = jnp.exp(jnp.where(first_half, jnp.broadcast_to(log_end[:, e0:e0 + 1], (L, 2 * P)),
                                      jnp.broadcast_to(log_end[:, e0 + 1:e0 + 2], (L, 2 * P))))
            x_end = (x_ref[:, sl].astype(F32) * w_end).astype(BF16)
            upd = lax.dot_general(bm, x_end, (((0,), (0,)), ((), ())), preferred_element_type=F32)
            tot_pair = jnp.where(first_half[0:1], jnp.broadcast_to(tot[:, e0:e0 + 1], (1, 2 * P)),
                                 jnp.broadcast_to(tot[:, e0 + 1:e0 + 2], (1, 2 * P)))
            state_ref[:, sl] = state_ref[:, sl] * jnp.exp(tot_pair) + upd


def _ssd_row_block(b, s, rev):
    n_lat = SEQ // SSD_CHUNK
    if rev:
        ctx = N_LAT // SSD_CHUNK + 2 * b + (1 - s)
        lat = n_lat * b + (n_lat + 1 - s)
    else:
        ctx = N_LAT // SSD_CHUNK + 2 * b + s
        lat = n_lat * b + (s - 2)
    return jnp.where(s < 2, ctx, lat)


def _ssd_scan(xconv, dt, alog_p, rev):
    n_lat = SEQ // SSD_CHUNK
    gps = SSD_GROUPS_PER_STEP
    rb = functools.partial(_ssd_row_block, rev=rev)

    def out_block(b, s):
        return _ssd_row_block(b, jnp.maximum(s, 2), rev)

    bw = gps * SSD_D_STATE
    b0 = SSD_D_INNER // bw
    c0 = (SSD_D_INNER + SSD_GN) // bw
    return pl.pallas_call(
        functools.partial(_ssd_kernel, rev=rev),
        grid=(BATCH, SSD_N_GROUPS // gps, n_lat + 2),
        in_specs=[pl.BlockSpec((SSD_CHUNK, gps * SSD_GROUP_W), lambda b, g, s: (rb(b, s), g)),
                  pl.BlockSpec((SSD_CHUNK, bw), lambda b, g, s: (rb(b, s), b0 + g)),
                  pl.BlockSpec((SSD_CHUNK, bw), lambda b, g, s: (rb(b, s), c0 + g)),
                  pl.BlockSpec((SSD_CHUNK, gps * 128), lambda b, g, s: (rb(b, s), g)),
                  pl.BlockSpec((1, gps * 128), lambda b, g, s: (0, g))],
        out_specs=pl.BlockSpec((SSD_CHUNK, gps * SSD_GROUP_W), lambda b, g, s: (out_block(b, s), g)),
        out_shape=jax.ShapeDtypeStruct((N_LAT, SSD_D_INNER), BF16),
        scratch_shapes=[pltpu.VMEM((SSD_D_STATE, gps * SSD_GROUP_W), F32)],
        compiler_params=_params(("parallel", "parallel", "arbitrary"), 32 << 20),
        name="ssd_scan_bwd" if rev else "ssd_scan_fwd",
    )(xconv, xconv, xconv, dt, alog_p)


def _ssd_gate_kernel(yf_ref, yb_ref, xs_ref, z_ref, d_ref, g_ref, o_ref):
    z = z_ref[...].astype(F32)
    y = yf_ref[...].astype(F32) + yb_ref[...].astype(F32) + d_ref[...] * xs_ref[...].astype(F32)
    y = y * (z * _sigmoid(z))
    o_ref[...] = (_rms(y) * g_ref[...]).astype(BF16)


def _ssd_gate(yf, yb, xconv, proj, d_exp, norm_g, bm=512):
    blk = lambda: pl.BlockSpec((bm, SSD_GROUP_W), lambda i, g: (i, g))
    vec = lambda: pl.BlockSpec((1, SSD_GROUP_W), lambda i, g: (0, g))
    return pl.pallas_call(
        _ssd_gate_kernel,
        grid=(N_LAT // bm, SSD_N_GROUPS),
        in_specs=[blk(), blk(), blk(), blk(), vec(), vec()],
        out_specs=blk(),
        out_shape=jax.ShapeDtypeStruct((N_LAT, SSD_D_INNER), BF16),
        compiler_params=_params(("parallel", "parallel"), 32 << 20),
        name="ssd_gate_norm",
    )(yf, yb, xconv, proj, d_exp, norm_g)


def _rg_conv_lat_kernel(x_ref, w_ref, b_ref, o_ref):
    w = w_ref[...]
    bias = b_ref[...]
    col = lax.broadcasted_iota(jnp.int32, (GRID_W, x_ref.shape[2]), 0)

    def row(r):
        return x_ref[r].astype(F32)

    def prev_col(v):
        return jnp.where(col >= 1, pltpu.roll(v, 1, axis=0), 0.0)

    def next_col(v):
        return jnp.where(col < GRID_W - 1, pltpu.roll(v, GRID_W - 1, axis=0), 0.0)

    for r in range(GRID_H):
        m2 = row(r - 2) if r >= 2 else prev_col(row(r - 2 + GRID_H))
        m1 = row(r - 1) if r >= 1 else prev_col(row(GRID_H - 1))
        p1 = row(r + 1) if r < GRID_H - 1 else next_col(row(0))
        o_ref[r] = m2 * w[0:1] + m1 * w[1:2] + row(r) * w[2:3] + p1 * w[3:4] + bias


def _rg_conv_lat(xr3, w, b, cb=512):
    return pl.pallas_call(
        _rg_conv_lat_kernel,
        grid=(BATCH, RG_WIDTH_P // cb),
        in_specs=[pl.BlockSpec((GRID_H, GRID_W, cb), lambda bb, j: (bb, 0, j)),
                  pl.BlockSpec((4, cb), lambda bb, j: (0, j)),
                  pl.BlockSpec((1, cb), lambda bb, j: (0, j))],
        out_specs=pl.BlockSpec((GRID_H, GRID_W, cb), lambda bb, j: (bb, 0, j)),
        out_shape=jax.ShapeDtypeStruct((BATCH * GRID_H, GRID_W, RG_WIDTH_P), F32),
        compiler_params=_params(("parallel", "parallel"), 32 << 20),
        name="rg_conv_lat",
    )(xr3, w, b)


def _rg_scan_kernel(xl_ref, xc_ref, wa_ref, wx_ref, ba_ref, bx_ref, lam_ref, h_ref, carry_ref, a_ref, b_ref, *, rev):
    s = pl.program_id(2)

    @pl.when(s == 0)
    def _():
        carry_ref[...] = jnp.zeros_like(carry_ref)

    x3 = jnp.where(s == 0, xc_ref[...], xl_ref[...])
    x2 = x3.reshape(GRID_H * RG_SEG, RG_CB)
    xb = x2.astype(BF16)
    neg_c_sp = -RG_C * _softplus(-lam_ref[...])
    for k in range(RG_CB // RG_BLOCK_P):
        sl = slice(k * RG_BLOCK_P, (k + 1) * RG_BLOCK_P)
        xk = xb[:, sl]
        r = _sigmoid(jnp.dot(xk, wa_ref[k], preferred_element_type=F32) + ba_ref[:, sl])
        i = _sigmoid(jnp.dot(xk, wx_ref[k], preferred_element_type=F32) + bx_ref[:, sl])
        log_a = neg_c_sp[:, sl] * r
        a_ref[:, :, sl] = jnp.exp(log_a).reshape(GRID_H, RG_SEG, RG_BLOCK_P)
        b_ref[:, :, sl] = (jnp.sqrt(_neg_expm1_2x(log_a)) * (i * x2[:, sl])).reshape(GRID_H, RG_SEG, RG_BLOCK_P)

    order = range(GRID_H - 1, -1, -1) if rev else range(GRID_H)
    p = None
    for r in order:
        a = a_ref[r]
        if p is None:
            p, hloc = a, b_ref[r]
        else:
            p, hloc = a * p, a * hloc + b_ref[r]
        a_ref[r] = p
        b_ref[r] = hloc
    seg_row = lax.broadcasted_iota(jnp.int32, (RG_SEG, RG_CB), 0)
    state = carry_ref[...]
    enter = jnp.zeros((RG_SEG, RG_CB), F32)
    for c in (range(RG_SEG - 1, -1, -1) if rev else range(RG_SEG)):
        enter = jnp.where(seg_row == c, state, enter)
        state = hloc[c:c + 1] + p[c:c + 1] * state
    carry_ref[...] = state

    @pl.when(s >= 1)
    def _():
        for r in range(GRID_H):
            h_ref[r] = b_ref[r] + a_ref[r] * enter


def _rg_scan(xl3, xc3, wa, wx, ba, bx, lam, rev):
    n_segblk = GRID_W // RG_SEG

    def seg_block(s):
        return jnp.clip((n_segblk - s) if rev else (s - 1), 0, n_segblk - 1)

    blk = (GRID_H, RG_SEG, RG_CB)
    nk = RG_CB // RG_BLOCK_P
    wspec = lambda: pl.BlockSpec((nk, RG_BLOCK_P, RG_BLOCK_P), lambda cb, b, s: (cb, 0, 0))
    vspec = lambda: pl.BlockSpec((1, RG_CB), lambda cb, b, s: (0, cb))
    return pl.pallas_call(
        functools.partial(_rg_scan_kernel, rev=rev),
        grid=(RG_WIDTH_P // RG_CB, BATCH, n_segblk + 1),
        in_specs=[pl.BlockSpec(blk, lambda cb, b, s: (b, seg_block(s), cb)),
                  pl.BlockSpec(blk, lambda cb, b, s: (b, 0, cb)),
                  wspec(), wspec(), vspec(), vspec(), vspec()],
        out_specs=pl.BlockSpec(blk, lambda cb, b, s: (b, seg_block(s), cb)),
        out_shape=jax.ShapeDtypeStruct((BATCH * GRID_H, GRID_W, RG_WIDTH_P), F32),
        scratch_shapes=[pltpu.VMEM((1, RG_CB), F32), pltpu.VMEM(blk, F32), pltpu.VMEM(blk, F32)],
        compiler_params=_params(("parallel", "parallel", "arbitrary"), 40 << 20),
        name="rg_scan_bwd" if rev else "rg_scan_fwd",
    )(xl3, xc3, wa, wx, ba, bx, lam)


def _rg_gate_kernel(g_ref, hf_ref, hb_ref, o_ref):
    g = g_ref[...].astype(F32)
    o_ref[...] = (jax.nn.gelu(g) * (hf_ref[...] + hb_ref[...])).astype(BF16)


def _rg_gate(gg, hf, hb, bm=512, bn=1536):
    return pl.pallas_call(
        _rg_gate_kernel,
        grid=(N_LAT // bm, RG_WIDTH_P // bn),
        in_specs=[pl.BlockSpec((bm, bn), lambda i, j: (i, j)),
                  pl.BlockSpec((bm, bn), lambda i, j: (i, j)),
                  pl.BlockSpec((bm, bn), lambda i, j: (i, j))],
        out_specs=pl.BlockSpec((bm, bn), lambda i, j: (i, j)),
        out_shape=jax.ShapeDtypeStruct((N_LAT, RG_WIDTH_P), BF16),
        compiler_params=_params(("parallel", "parallel"), 32 << 20),
        name="rg_gate",
    )(gg, hf, hb)


def _pad_cols_kernel(w_ref, o_ref):
    w = w_ref[...]
    zeros = jnp.zeros((w.shape[0], RG_BLOCK_P - RG_BLOCK), BF16)
    for k in range(RG_N_BLOCKS):
        blk = w[:, k * RG_BLOCK:(k + 1) * RG_BLOCK].astype(BF16)
        o_ref[:, k * RG_BLOCK_P:(k + 1) * RG_BLOCK_P] = jnp.concatenate([blk, zeros], axis=1)


def _pad_cols(w, col0, bm=512):
    rows = w.shape[0]
    return pl.pallas_call(
        _pad_cols_kernel,
        grid=(rows // bm,),
        in_specs=[pl.BlockSpec((pl.Element(bm), pl.Element(RG_WIDTH)), lambda i: (i * bm, col0))],
        out_specs=pl.BlockSpec((bm, RG_WIDTH_P), lambda i: (i, 0)),
        out_shape=jax.ShapeDtypeStruct((rows, RG_WIDTH_P), BF16),
        compiler_params=_params(("parallel",), 48 << 20),
        name="pad_rg_cols",
    )(w)


def _pad_rg(a, axis):
    axis = axis % a.ndim
    shape = a.shape[:axis] + (RG_N_BLOCKS, RG_BLOCK) + a.shape[axis + 1:]
    pad = [(0, 0)] * (a.ndim + 1)
    pad[axis + 1] = (0, RG_BLOCK_P - RG_BLOCK)
    out = jnp.pad(a.reshape(shape), pad)
    return out.reshape(a.shape[:axis] + (RG_WIDTH_P,) + a.shape[axis + 1:])


def _group_lanes(a):
    lead = a.shape[:-1]
    e = SSD_HEADS_PER_GROUP
    t = a.reshape(lead + (2, SSD_N_GROUPS, e))
    t = jnp.moveaxis(t, -3, -2).reshape(lead + (SSD_N_GROUPS, 2 * e))
    t = jnp.pad(t, [(0, 0)] * len(lead) + [(0, 0), (0, 128 - 2 * e)])
    return t.reshape(lead + (DT_W,))


def kernel(x, c, ctx, c_ctx, w_ada, b_ada, norm_g, ffn_w_up, ffn_w_down, w_in, ssd_conv_w, ssd_conv_b, ssd_dt_bias,
           ssd_a_log, ssd_d, ssd_norm_g, w_ssd_out, rg_conv_w, rg_conv_b, rg_w_a, rg_b_a, rg_w_x, rg_b_x, rg_lam,
           w_rg_out, w_out):
    l = 0
    g = norm_g[l]

    wi = w_in[l]
    w_dt = _group_lanes(wi[:, _S2:_S3]).astype(BF16)
    w_gg = _pad_cols(wi, _S3)
    w_xr = _pad_cols(wi, _S4)
    dt_bias_p = _group_lanes(ssd_dt_bias[l]).reshape(1, DT_W)
    alog_p = _group_lanes(ssd_a_log[l].reshape(2 * SSD_N_HEADS)).reshape(1, DT_W)
    d_exp = jnp.repeat(ssd_d[l], SSD_HEAD_DIM).reshape(1, SSD_D_INNER)
    ssd_ng = ssd_norm_g[l].reshape(1, SSD_D_INNER)
    ssd_cw, ssd_cb = ssd_conv_w[l], ssd_conv_b[l].reshape(1, SSD_CONV_DIM)
    rg_cw, rg_cb = _pad_rg(rg_conv_w[l], 1), _pad_rg(rg_conv_b[l], 0).reshape(1, RG_WIDTH_P)
    pad_w = lambda w: jnp.pad(w, ((0, 0), (0, RG_BLOCK_P - RG_BLOCK), (0, RG_BLOCK_P - RG_BLOCK))).astype(BF16)
    rg_wa = [pad_w(rg_w_a[l, d]) for d in range(2)]
    rg_wx = [pad_w(rg_w_x[l, d]) for d in range(2)]
    rg_ba = [_pad_rg(rg_b_a[l, d], 0).reshape(1, RG_WIDTH_P) for d in range(2)]
    rg_bx = [_pad_rg(rg_b_x[l, d], 0).reshape(1, RG_WIDTH_P) for d in range(2)]
    rg_lm = [_pad_rg(rg_lam[l, d], 0).reshape(1, RG_WIDTH_P) for d in range(2)]
    w_rg_o = _pad_rg(w_rg_out[l].astype(BF16), 0)
    w_o = w_out[l].astype(BF16)

    c8 = jnp.concatenate([c, c_ctx[None], jnp.zeros((8 - BATCH - 1, D_MODEL), F32)], axis=0)
    mod3 = _ada(c8, w_ada[l], b_ada[l].reshape(1, -1)).reshape(8, 1, N_MOD * D_MODEL)

    h0 = (x.reshape(N_LAT, D_MODEL), ctx.reshape(N_CTX, D_MODEL))

    u1 = _normmod(h0, g[0], mod3, 0, N_TOK)
    y1 = _ffn(u1, ffn_w_up, ffn_w_down, (l, 0))
    h1, u2 = _residual(h0, y1, g[1], mod3, 0, MACARON_W, nxt=(g[2], 1))

    direct = functools.partial(_matmul, u2, w_in, w_lead=(l,), out_dtype=BF16, single_buffer_x=True)
    z = direct(rows=N_LAT, bm=N_LAT // 4, bn=512, w_col0=0, n=_S1, name="in_proj_z")
    xbc = direct(rows=N_TOK, bm=N_TOK // 4, bn=512, w_col0=_S1, n=SSD_CONV_DIM, name="in_proj_xbc")
    mg, w_ssd_o = direct(rows=N_LAT, bm=N_LAT // 4, bn=256, w_col0=_S5, n=2 * D_MODEL, side=(w_ssd_out, (l,)),
                         name="in_proj_merge_gates")
    proj = functools.partial(_matmul, u2, bm=1024, bn=1024)
    gg = proj(w_gg, rows=N_LAT, out_dtype=BF16, name="in_proj_gelu_gate")
    xr = proj(w_xr, rows=N_TOK, out_dtype=BF16, name="in_proj_rg")
    dt = proj(w_dt, rows=N_TOK, out_dtype=F32, softplus_bias=dt_bias_p, name="in_proj_dt")

    xconv = _conv_rows(xbc, ssd_cw, ssd_cb, row_blk0=0, n_blk=N_TOK // CONV_BM, silu=True, out_dtype=BF16)
    y_f = _ssd_scan(xconv, dt, alog_p, rev=False)
    y_b = _ssd_scan(xconv, dt, alog_p, rev=True)
    y_n = _ssd_gate(y_f, y_b, xconv, z, d_exp, ssd_ng)

    xr_lat = _rg_conv_lat(xr.reshape(N_TOK // GRID_W, GRID_W, RG_WIDTH_P), rg_cw, rg_cb)
    xr_ctx = _conv_rows(xr, rg_cw, rg_cb, row_blk0=N_LAT // CONV_BM, n_blk=N_CTX // CONV_BM, silu=False,
                        out_dtype=F32)
    xr_ctx = xr_ctx.reshape(BATCH, RG_SEG, GRID_H, RG_WIDTH_P).transpose(0, 2, 1, 3)
    xr_ctx = xr_ctx.reshape(BATCH * GRID_H, RG_SEG, RG_WIDTH_P)
    h_f = _rg_scan(xr_lat, xr_ctx, rg_wa[0], rg_wx[0], rg_ba[0], rg_bx[0], rg_lm[0], rev=False)
    h_b = _rg_scan(xr_lat, xr_ctx, rg_wa[1], rg_wx[1], rg_ba[1], rg_bx[1], rg_lm[1], rev=True)
    r_in = _rg_gate(gg, h_f.reshape(N_LAT, RG_WIDTH_P), h_b.reshape(N_LAT, RG_WIDTH_P))

    m1 = _matmul(y_n, w_ssd_o, rows=N_LAT, bm=1024, bn=512, out_dtype=F32, gate=(mg, 0), single_buffer_x=True,
                 name="ssd_out_proj")
    m2 = _matmul(r_in, w_rg_o, rows=N_LAT, bm=1024, bn=512, out_dtype=BF16, gate=(mg, D_MODEL // 512), prev=m1,
                 name="rg_out_proj")
    m3 = _matmul(m2, w_o, rows=N_LAT, bm=1024, bn=512, out_dtype=F32, name="out_proj")
    h2, u3 = _residual(h1, m3, g[3], mod3, 1, 1.0, nxt=(g[4], 2))

    y3 = _ffn(u3, ffn_w_up, ffn_w_down, (l, 1))
    out = _residual(h2, y3, g[5], mod3, 2, MACARON_W)
    return out.reshape(BATCH, SEQ, D_MODEL)
```

```python
import functools
import math

import jax
import jax.numpy as jnp
from jax import lax
from jax.experimental import pallas as pl
from jax.experimental.pallas import tpu as pltpu

F32 = jnp.float32
BF16 = jnp.bfloat16

D_MODEL = 4096
BATCH = 4
SEQ = 2048
GRID_W = 64
GRID_H = SEQ // GRID_W
CTX_LEN = 256
N_MOD = 9
EPS = 1e-6
MACARON_W = 0.5

N_LAT = BATCH * SEQ
N_CTX = BATCH * CTX_LEN
N_TOK = N_LAT + N_CTX

SSD_D_INNER = 2 * D_MODEL
SSD_HEAD_DIM = 64
SSD_N_HEADS = SSD_D_INNER // SSD_HEAD_DIM
SSD_N_GROUPS = 8
SSD_HEADS_PER_GROUP = SSD_N_HEADS // SSD_N_GROUPS
SSD_GROUP_W = SSD_D_INNER // SSD_N_GROUPS
SSD_D_STATE = 128
SSD_GN = SSD_N_GROUPS * SSD_D_STATE
SSD_CONV_DIM = SSD_D_INNER + 2 * SSD_GN
SSD_CHUNK = 128
SSD_GROUPS_PER_STEP = 8

RG_WIDTH = 5376
RG_N_BLOCKS = 16
RG_BLOCK = RG_WIDTH // RG_N_BLOCKS
RG_BLOCK_P = 384
RG_WIDTH_P = RG_N_BLOCKS * RG_BLOCK_P
RG_C = 8.0
RG_SEG = 8
RG_CB = 4 * RG_BLOCK_P

D_FF = 11008

_S1 = SSD_D_INNER
_S2 = _S1 + SSD_CONV_DIM
_S3 = _S2 + 2 * SSD_N_HEADS
_S4 = _S3 + RG_WIDTH
_S5 = _S4 + RG_WIDTH

DT_W = SSD_N_GROUPS * 128

VMEM_CAP = 56 * 1024 * 1024


def _params(sem, vmem_bytes):
    return pltpu.CompilerParams(dimension_semantics=sem,
                                vmem_limit_bytes=int(min(VMEM_CAP, max(32 << 20, vmem_bytes))))


def _softplus(x):
    return jnp.maximum(x, 0.0) + jnp.log1p(jnp.exp(-jnp.abs(x)))


def _sigmoid(x):
    return 0.5 * jnp.tanh(0.5 * x) + 0.5


def _neg_expm1_2x(x):
    t = jnp.tanh(x)
    return (t + t) / (t - 1.0)


ADA_BN = 512


def _ada_kernel(c_ref, w_ref, b_ref, o_ref):
    c = c_ref[...]
    sc = (c * _sigmoid(c)).astype(BF16)
    o_ref[...] = jnp.dot(sc, w_ref[...].astype(BF16), preferred_element_type=F32) + b_ref[...]


def _ada(c8, w_ada, b_ada):
    n = w_ada.shape[1]
    return pl.pallas_call(
        _ada_kernel,
        grid=(n // ADA_BN,),
        in_specs=[pl.BlockSpec((8, D_MODEL), lambda j: (0, 0)),
                  pl.BlockSpec((D_MODEL, ADA_BN), lambda j: (0, j)),
                  pl.BlockSpec((1, ADA_BN), lambda j: (0, j))],
        out_specs=pl.BlockSpec((8, ADA_BN), lambda j: (0, j)),
        out_shape=jax.ShapeDtypeStruct((8, n), F32),
        compiler_params=_params(("parallel",), 2 * D_MODEL * ADA_BN * 4 + D_MODEL * ADA_BN * 4 + (4 << 20)),
        name="ada_mod",
    )(c8, w_ada, b_ada)


ROW_BM = 256


def _mod_row(i):
    return jnp.where(i < N_LAT // ROW_BM, i // (SEQ // ROW_BM), BATCH)


def _mod_spec(slot):
    return pl.BlockSpec((None, 1, D_MODEL), lambda i: (_mod_row(i), 0, slot))


def _vec_spec():
    return pl.BlockSpec((1, D_MODEL), lambda i: (0, 0))


def _row_spec():
    return pl.BlockSpec((ROW_BM, D_MODEL), lambda i: (i, 0))


N_LAT_BLK = N_LAT // ROW_BM


def _token_specs(h):
    if not isinstance(h, tuple):
        return [_row_spec()], [h]
    return [pl.BlockSpec((ROW_BM, D_MODEL), lambda i: (jnp.minimum(i, N_LAT_BLK - 1), 0)),
            pl.BlockSpec((ROW_BM, D_MODEL), lambda i: (jnp.maximum(i - N_LAT_BLK, 0), 0))], list(h)


def _load_tokens(refs):
    if len(refs) == 1:
        return refs[0][...]
    return jnp.where(pl.program_id(0) < N_LAT_BLK, refs[0][...], refs[1][...])


def _rms(x):
    return x * lax.rsqrt(jnp.mean(x * x, axis=-1, keepdims=True) + EPS)


def _normmod_kernel(*refs, n_h):
    g_ref, sc_ref, sh_ref, u_ref = refs[n_h:]
    u = _rms(_load_tokens(refs[:n_h])) * g_ref[...]
    u_ref[...] = (u * (1.0 + sc_ref[...]) + sh_ref[...]).astype(BF16)


def _normmod(h, g, mod3, slot, rows):
    h_specs, h_args = _token_specs(h)
    return pl.pallas_call(
        functools.partial(_normmod_kernel, n_h=len(h_args)),
        grid=(rows // ROW_BM,),
        in_specs=h_specs + [_vec_spec(), _mod_spec(3 * slot + 1), _mod_spec(3 * slot)],
        out_specs=_row_spec(),
        out_shape=jax.ShapeDtypeStruct((rows, D_MODEL), BF16),
        compiler_params=_params(("parallel",), 40 << 20),
        name="normmod",
    )(*h_args, g.reshape(1, D_MODEL), mod3, mod3)


def _res_kernel(*refs, n_h, coef, with_next):
    y_ref, gpost_ref, gate_ref = refs[n_h:n_h + 3]
    rest = refs[n_h + 3:]
    h2 = _load_tokens(refs[:n_h]) + (coef * gate_ref[...]) * (_rms(y_ref[...].astype(F32)) * gpost_ref[...])
    if with_next:
        gn_ref, sc_ref, sh_ref, ho_ref, u_ref = rest
        ho_ref[...] = h2
        u = _rms(h2) * gn_ref[...]
        u_ref[...] = (u * (1.0 + sc_ref[...]) + sh_ref[...]).astype(BF16)
    else:
        (ho_ref,) = rest
        ho_ref[...] = h2


def _residual(h, y, g_post, mod3, slot, coef, nxt=None):
    rows = y.shape[0]
    h_specs, h_args = _token_specs(h)
    in_specs = h_specs + [_row_spec(), _vec_spec(), _mod_spec(3 * slot + 2)]
    args = h_args + [y, g_post.reshape(1, D_MODEL), mod3]
    out_specs = [_row_spec()]
    out_shape = [jax.ShapeDtypeStruct((rows, D_MODEL), F32)]
    if nxt is not None:
        g_next, slot_next = nxt
        in_specs += [_vec_spec(), _mod_spec(3 * slot_next + 1), _mod_spec(3 * slot_next)]
        args += [g_next.reshape(1, D_MODEL), mod3, mod3]
        out_specs.append(_row_spec())
        out_shape.append(jax.ShapeDtypeStruct((rows, D_MODEL), BF16))
    out = pl.pallas_call(
        functools.partial(_res_kernel, n_h=len(h_args), coef=coef, with_next=nxt is not None),
        grid=(rows // ROW_BM,),
        in_specs=in_specs, out_specs=out_specs, out_shape=out_shape,
        compiler_params=_params(("parallel",), 48 << 20),
        name="residual_norm",
    )(*args)
    return out if nxt is not None else out[0]


def _w_spec(lead, k, bn, col_blk0):
    squeezed = (None,) * len(lead)
    return pl.BlockSpec(squeezed + (k, bn), lambda i, j: tuple(lead) + (0, col_blk0 + j))


def _side_cast_specs(side, n_col_blocks):
    arr, lead = side
    rows, cols = arr.shape[-2:]
    r = rows // n_col_blocks

    def blk(i, j):
        return jnp.where(i == 0, j, n_col_blocks - 1)

    squeezed = (None,) * len(lead)
    in_spec = pl.BlockSpec(squeezed + (r, cols), lambda i, j: tuple(lead) + (blk(i, j), 0))
    out_spec = pl.BlockSpec((r, cols), lambda i, j: (blk(i, j), 0))
    return in_spec, out_spec, jax.ShapeDtypeStruct((rows, cols), BF16), 2 * r * cols * (4 + 2)


def _side_cast(side_ref, side_o_ref):
    @pl.when(pl.program_id(0) == 0)
    def _():
        side_o_ref[...] = side_ref[...].astype(BF16)


def _up_kernel(x_ref, wg_ref, wu_ref, side_ref, o_ref, side_o_ref):
    x = x_ref[...]
    g = jnp.dot(x, wg_ref[...].astype(BF16), preferred_element_type=F32)
    u = jnp.dot(x, wu_ref[...].astype(BF16), preferred_element_type=F32)
    o_ref[...] = (g * _sigmoid(g) * u).astype(BF16)
    _side_cast(side_ref, side_o_ref)


FFN_UP_BN = 256


def _ffn_up(u, w_up, w_down, lead):
    rows, k = u.shape
    bm, bn = rows // 4, FFN_UP_BN
    side_in, side_out, side_shape, side_bytes = _side_cast_specs((w_down, lead), D_FF // bn)
    vmem = (bm * k * 2 + 4 * k * bn * 4 + 2 * k * bn * 2 + 2 * bm * bn * 2 + 3 * bm * bn * 4 + side_bytes
            + (4 << 20))
    return pl.pallas_call(
        _up_kernel,
        grid=(rows // bm, D_FF // bn),
        in_specs=[pl.BlockSpec((bm, k), lambda i, j: (i, 0), pipeline_mode=pl.Buffered(1)),
                  _w_spec(lead, k, bn, 0),
                  _w_spec(lead, k, bn, D_FF // bn),
                  side_in],
        out_specs=[pl.BlockSpec((bm, bn), lambda i, j: (i, j)), side_out],
        out_shape=[jax.ShapeDtypeStruct((rows, D_FF), BF16), side_shape],
        compiler_params=_params(("arbitrary", "arbitrary"), vmem),
        name="ffn_up_swiglu",
    )(u, w_up, w_up, w_down)


def _mm_kernel(*refs, mode, has_side):
    refs = list(refs)
    if has_side:
        side_o_ref = refs.pop()
    o_ref = refs.pop()
    if has_side:
        _side_cast(refs.pop(), side_o_ref)
    x_ref, w_ref, *rest = refs
    acc = jnp.dot(x_ref[...], w_ref[...].astype(BF16), preferred_element_type=F32)
    if mode == "softplus_bias":
        (b_ref,) = rest
        acc = _softplus(acc + b_ref[...])
    elif mode == "gate":
        (g_ref,) = rest
        acc = _sigmoid(g_ref[...].astype(F32)) * acc
    elif mode == "gate_add":
        g_ref, p_ref = rest
        acc = p_ref[...] + _sigmoid(g_ref[...].astype(F32)) * acc
    o_ref[...] = acc.astype(o_ref.dtype)


def _matmul(x, w, *, rows, bm, bn, out_dtype, gate=None, prev=None, softplus_bias=None, single_buffer_x=False,
            w_lead=(), w_col0=0, n=None, side=None, name="matmul"):
    k = x.shape[1]
    n = w.shape[-1] if n is None else n
    mode = "plain" if gate is None else ("gate" if prev is None else "gate_add")
    x_kwargs = dict(pipeline_mode=pl.Buffered(1)) if single_buffer_x else {}
    in_specs = [pl.BlockSpec((bm, k), lambda i, j: (i, 0), **x_kwargs),
                _w_spec(w_lead, k, bn, w_col0 // bn)]
    args = [x, w]
    if softplus_bias is not None:
        mode = "softplus_bias"
        in_specs.append(pl.BlockSpec((1, bn), lambda i, j: (0, j)))
        args.append(softplus_bias)
    if gate is not None:
        g_arr, g_blk0 = gate
        in_specs.append(pl.BlockSpec((bm, bn), lambda i, j: (i, g_blk0 + j)))
        args.append(g_arr)
    if prev is not None:
        in_specs.append(pl.BlockSpec((bm, bn), lambda i, j: (i, j)))
        args.append(prev)
    out_specs = [pl.BlockSpec((bm, bn), lambda i, j: (i, j))]
    out_shape = [jax.ShapeDtypeStruct((rows, n), out_dtype)]
    xbuf = 1 if single_buffer_x else 2
    w_bytes = 2 * k * bn * w.dtype.itemsize + (k * bn * 2 if w.dtype != BF16 else 0)
    vmem = xbuf * bm * k * 2 + w_bytes + 5 * bm * bn * 4 + (4 << 20)
    if side is not None:
        side_in, side_out, side_shape, side_bytes = _side_cast_specs(side, n // bn)
        in_specs.append(side_in)
        args.append(side[0])
        out_specs.append(side_out)
        out_shape.append(side_shape)
        vmem += side_bytes
    out = pl.pallas_call(
        functools.partial(_mm_kernel, mode=mode, has_side=side is not None),
        grid=(rows // bm, n // bn),
        in_specs=in_specs,
        out_specs=out_specs,
        out_shape=out_shape,
        compiler_params=_params(("arbitrary", "arbitrary") if side is not None else ("parallel", "parallel"), vmem),
        name=name,
    )(*args)
    return out if side is not None else out[0]


def _ffn(u, w_up, w_down, lead):
    act, wd = _ffn_up(u, w_up, w_down, lead)
    return _matmul(act, wd, rows=act.shape[0], bm=1024, bn=512, out_dtype=BF16, single_buffer_x=True,
                   name="ffn_down")


CONV_BM = CTX_LEN
CONV_HALO = 16
CONV_CB = 2048


def _conv_rows_kernel(x_ref, prev_ref, next_ref, w_ref, b_ref, o_ref, *, row_blk0, silu):
    i = row_blk0 + pl.program_id(0)
    lat_blocks = N_LAT // CONV_BM
    per_seq = SEQ // CONV_BM
    is_ctx = i >= lat_blocks
    starts = jnp.logical_or(is_ctx, i % per_seq == 0)
    ends = jnp.logical_or(is_ctx, i % per_seq == per_seq - 1)
    x = x_ref[...].astype(F32)
    before = jnp.where(starts, 0.0, prev_ref[...].astype(F32)[CONV_HALO - 8:])
    after = jnp.where(ends, 0.0, next_ref[...].astype(F32)[:8])
    t = x.shape[0]
    w = w_ref[...]
    bias = b_ref[...]

    def taps(v):
        acc = v * w[2:3] + bias
        acc += pltpu.roll(v, 2, axis=0) * w[0:1]
        acc += pltpu.roll(v, 1, axis=0) * w[1:2]
        acc += pltpu.roll(v, v.shape[0] - 1, axis=0) * w[3:4]
        return acc * _sigmoid(acc) if silu else acc

    edge = 16
    top = taps(jnp.concatenate([before, x[:edge + 8]], axis=0))[8:8 + edge]
    bottom = taps(jnp.concatenate([x[t - edge - 8:], after], axis=0))[8:8 + edge]
    o_ref[...] = taps(x).astype(o_ref.dtype)
    o_ref[0:edge] = top.astype(o_ref.dtype)
    o_ref[t - edge:t] = bottom.astype(o_ref.dtype)


def _conv_rows(src, w, b, *, row_blk0, n_blk, silu, out_dtype):
    width = src.shape[1]
    halo_per_blk = CONV_BM // CONV_HALO
    last_halo = src.shape[0] // CONV_HALO - 1
    return pl.pallas_call(
        functools.partial(_conv_rows_kernel, row_blk0=row_blk0, silu=silu),
        grid=(n_blk, width // CONV_CB),
        in_specs=[pl.BlockSpec((CONV_BM, CONV_CB), lambda i, j: (row_blk0 + i, j)),
                  pl.BlockSpec((CONV_HALO, CONV_CB),
                               lambda i, j: (jnp.maximum((row_blk0 + i) * halo_per_blk - 1, 0), j)),
                  pl.BlockSpec((CONV_HALO, CONV_CB),
                               lambda i, j: (jnp.minimum((row_blk0 + i + 1) * halo_per_blk, last_halo), j)),
                  pl.BlockSpec((4, CONV_CB), lambda i, j: (0, j)),
                  pl.BlockSpec((1, CONV_CB), lambda i, j: (0, j))],
        out_specs=pl.BlockSpec((CONV_BM, CONV_CB), lambda i, j: (i, j)),
        out_shape=jax.ShapeDtypeStruct((n_blk * CONV_BM, width), out_dtype),
        compiler_params=_params(("parallel", "parallel"), 32 << 20),
        name="conv_rows",
    )(src, src, src, w, b)


def _split3(v):
    hi = v.astype(BF16)
    r1 = v - hi.astype(F32)
    mid = r1.astype(BF16)
    lo = (r1 - mid.astype(F32)).astype(BF16)
    return hi, mid, lo


def _dot_exact_rhs(lhs_bf16, v, terms=3):
    out = None
    for piece in _split3(v)[:terms]:
        d = jnp.dot(lhs_bf16, piece, preferred_element_type=F32)
        out = d if out is None else out + d
    return out


def _ssd_kernel(x_ref, b_ref, c_ref, dt_ref, alog_ref, y_ref, state_ref, *, rev):
    L = SSD_CHUNK
    E = SSD_HEADS_PER_GROUP
    P = SSD_HEAD_DIM
    GW = SSD_GROUP_W
    off = E if rev else 0
    s = pl.program_id(2)

    @pl.when(s == 0)
    def _():
        state_ref[...] = jnp.zeros_like(state_ref)

    ii = lax.broadcasted_iota(jnp.int32, (L, L), 0)
    jj = lax.broadcasted_iota(jnp.int32, (L, L), 1)
    tri = (jj >= ii) if rev else (jj <= ii)
    tri_b = jnp.where(tri, 1.0, 0.0).astype(BF16)
    first_half = lax.broadcasted_iota(jnp.int32, (L, 2 * P), 1) < P

    groups = []
    for gi in range(SSD_GROUPS_PER_STEP):
        tile = slice(gi * 128, (gi + 1) * 128)
        dt = dt_ref[:, tile]
        da = dt * (-jnp.exp(alog_ref[:, tile]))
        cs = _dot_exact_rhs(tri_b, da)
        tot = cs[0:1] if rev else cs[L - 1:L]
        log_end = jnp.log(dt) + (tot - cs)
        groups.append((dt, cs, tot, log_end, b_ref[:, tile], c_ref[:, tile]))

    @pl.when(s >= 2)
    def _():
        for gi, (dt, cs, tot, log_end, bm, cm) in enumerate(groups):
            cs_t = cs.T
            dt_t = dt.T
            cb = lax.dot_general(cm, bm, (((1,), (1,)), ((), ())), preferred_element_type=F32)
            for pair in range(E // 2):
                sl = slice(gi * GW + pair * 2 * P, gi * GW + (pair + 1) * 2 * P)
                xp = x_ref[:, sl]
                halves, cols = [], []
                for e in (off + 2 * pair, off + 2 * pair + 1):
                    col = jnp.broadcast_to(cs[:, e:e + 1], (L, L))
                    cols.append(col)
                    lmat = jnp.exp(jnp.where(tri, col - cs_t[e:e + 1, :], -1e30))
                    m = (cb * lmat * dt_t[e:e + 1, :]).astype(BF16)
                    halves.append(jnp.dot(m, xp, preferred_element_type=F32))
                from_start = jnp.exp(jnp.where(first_half, cols[0], cols[1]))
                y_off = jnp.dot(cm, state_ref[:, sl].astype(BF16), preferred_element_type=F32) * from_start
                y_ref[:, sl] = (jnp.where(first_half, halves[0], halves[1]) + y_off).astype(y_ref.dtype)

    for gi, (dt, cs, tot, log_end, bm, cm) in enumerate(groups):
        for pair in range(E // 2):
            sl = slice(gi * GW + pair * 2 * P, gi * GW + (pair + 1) * 2 * P)
            e0 = off + 2 * pair
            w_end = jnp.exp(jnp.where(first_half, jnp.broadcast_to(log_end[:, e0:e0 + 1], (L, 2 * P)),
                                      jnp.broadcast_to(log_end[:, e0 + 1:e0 + 2], (L, 2 * P))))
            x_end = (x_ref[:, sl].astype(F32) * w_end).astype(BF16)
            upd = lax.dot_general(bm, x_end, (((0,), (0,)), ((), ())), preferred_element_type=F32)
            tot_pair = jnp.where(first_half[0:1], jnp.broadcast_to(tot[:, e0:e0 + 1], (1, 2 * P)),
                                 jnp.broadcast_to(tot[:, e0 + 1:e0 + 2], (1, 2 * P)))
            state_ref[:, sl] = state_ref[:, sl] * jnp.exp(tot_pair) + upd


def _ssd_row_block(b, s, rev):
    n_lat = SEQ // SSD_CHUNK
    if rev:
        ctx = N_LAT // SSD_CHUNK + 2 * b + (1 - s)
        lat = n_lat * b + (n_lat + 1 - s)
    else:
        ctx = N_LAT // SSD_CHUNK + 2 * b + s
        lat = n_lat * b + (s - 2)
    return jnp.where(s < 2, ctx, lat)


def _ssd_scan(xconv, dt, alog_p, rev):
    n_lat = SEQ // SSD_CHUNK
    gps = SSD_GROUPS_PER_STEP
    rb = functools.partial(_ssd_row_block, rev=rev)

    def out_block(b, s):
        return _ssd_row_block(b, jnp.maximum(s, 2), rev)

    bw = gps * SSD_D_STATE
    b0 = SSD_D_INNER // bw
    c0 = (SSD_D_INNER + SSD_GN) // bw
    return pl.pallas_call(
        functools.partial(_ssd_kernel, rev=rev),
        grid=(BATCH, SSD_N_GROUPS // gps, n_lat + 2),
        in_specs=[pl.BlockSpec((SSD_CHUNK, gps * SSD_GROUP_W), lambda b, g, s: (rb(b, s), g)),
                  pl.BlockSpec((SSD_CHUNK, bw), lambda b, g, s: (rb(b, s), b0 + g)),
                  pl.BlockSpec((SSD_CHUNK, bw), lambda b, g, s: (rb(b, s), c0 + g)),
                  pl.BlockSpec((SSD_CHUNK, gps * 128), lambda b, g, s: (rb(b, s), g)),
                  pl.BlockSpec((1, gps * 128), lambda b, g, s: (0, g))],
        out_specs=pl.BlockSpec((SSD_CHUNK, gps * SSD_GROUP_W), lambda b, g, s: (out_block(b, s), g)),
        out_shape=jax.ShapeDtypeStruct((N_LAT, SSD_D_INNER), BF16),
        scratch_shapes=[pltpu.VMEM((SSD_D_STATE, gps * SSD_GROUP_W), F32)],
        compiler_params=_params(("parallel", "parallel", "arbitrary"), 32 << 20),
        name="ssd_scan_bwd" if rev else "ssd_scan_fwd",
    )(xconv, xconv, xconv, dt, alog_p)


def _ssd_gate_kernel(yf_ref, yb_ref, xs_ref, z_ref, d_ref, g_ref, o_ref):
    z = z_ref[...].astype(F32)
    y = yf_ref[...].astype(F32) + yb_ref[...].astype(F32) + d_ref[...] * xs_ref[...].astype(F32)
    y = y * (z * _sigmoid(z))
    o_ref[...] = (_rms(y) * g_ref[...]).astype(BF16)


def _ssd_gate(yf, yb, xconv, proj, d_exp, norm_g, bm=512):
    blk = lambda: pl.BlockSpec((bm, SSD_GROUP_W), lambda i, g: (i, g))
    vec = lambda: pl.BlockSpec((1, SSD_GROUP_W), lambda i, g: (0, g))
    return pl.pallas_call(
        _ssd_gate_kernel,
        grid=(N_LAT // bm, SSD_N_GROUPS),
        in_specs=[blk(), blk(), blk(), blk(), vec(), vec()],
        out_specs=blk(),
        out_shape=jax.ShapeDtypeStruct((N_LAT, SSD_D_INNER), BF16),
        compiler_params=_params(("parallel", "parallel"), 32 << 20),
        name="ssd_gate_norm",
    )(yf, yb, xconv, proj, d_exp, norm_g)


def _rg_conv_lat_kernel(x_ref, w_ref, b_ref, o_ref):
    w = w_ref[...]
    bias = b_ref[...]
    col = lax.broadcasted_iota(jnp.int32, (GRID_W, x_ref.shape[2]), 0)

    def row(r):
        return x_ref[r].astype(F32)

    def prev_col(v):
        return jnp.where(col >= 1, pltpu.roll(v, 1, axis=0), 0.0)

    def next_col(v):
        return jnp.where(col < GRID_W - 1, pltpu.roll(v, GRID_W - 1, axis=0), 0.0)

    for r in range(GRID_H):
        m2 = row(r - 2) if r >= 2 else prev_col(row(r - 2 + GRID_H))
        m1 = row(r - 1) if r >= 1 else prev_col(row(GRID_H - 1))
        p1 = row(r + 1) if r < GRID_H - 1 else next_col(row(0))
        o_ref[r] = m2 * w[0:1] + m1 * w[1:2] + row(r) * w[2:3] + p1 * w[3:4] + bias


def _rg_conv_lat(xr3, w, b, cb=512):
    return pl.pallas_call(
        _rg_conv_lat_kernel,
        grid=(BATCH, RG_WIDTH_P // cb),
        in_specs=[pl.BlockSpec((GRID_H, GRID_W, cb), lambda bb, j: (bb, 0, j)),
                  pl.BlockSpec((4, cb), lambda bb, j: (0, j)),
                  pl.BlockSpec((1, cb), lambda bb, j: (0, j))],
        out_specs=pl.BlockSpec((GRID_H, GRID_W, cb), lambda bb, j: (bb, 0, j)),
        out_shape=jax.ShapeDtypeStruct((BATCH * GRID_H, GRID_W, RG_WIDTH_P), F32),
        compiler_params=_params(("parallel", "parallel"), 32 << 20),
        name="rg_conv_lat",
    )(xr3, w, b)


def _rg_scan_kernel(xl_ref, xc_ref, wa_ref, wx_ref, ba_ref, bx_ref, lam_ref, h_ref, carry_ref, a_ref, b_ref, *, rev):
    s = pl.program_id(2)

    @pl.when(s == 0)
    def _():
        carry_ref[...] = jnp.zeros_like(carry_ref)

    x3 = jnp.where(s == 0, xc_ref[...], xl_ref[...])
    x2 = x3.reshape(GRID_H * RG_SEG, RG_CB)
    xb = x2.astype(BF16)
    neg_c_sp = -RG_C * _softplus(-lam_ref[...])
    for k in range(RG_CB // RG_BLOCK_P):
        sl = slice(k * RG_BLOCK_P, (k + 1) * RG_BLOCK_P)
        xk = xb[:, sl]
        r = _sigmoid(jnp.dot(xk, wa_ref[k], preferred_element_type=F32) + ba_ref[:, sl])
        i = _sigmoid(jnp.dot(xk, wx_ref[k], preferred_element_type=F32) + bx_ref[:, sl])
        log_a = neg_c_sp[:, sl] * r
        a_ref[:, :, sl] = jnp.exp(log_a).reshape(GRID_H, RG_SEG, RG_BLOCK_P)
        q = _neg_expm1_2x(log_a)
        root = jnp.where(q > 0.0, q * lax.rsqrt(q), 0.0)
        b_ref[:, :, sl] = (root * (i * x2[:, sl])).reshape(GRID_H, RG_SEG, RG_BLOCK_P)

    order = range(GRID_H - 1, -1, -1) if rev else range(GRID_H)
    p = None
    for r in order:
        a = a_ref[r]
        if p is None:
            p, hloc = a, b_ref[r]
        else:
            p, hloc = a * p, a * hloc + b_ref[r]
        a_ref[r] = p
        b_ref[r] = hloc
    seg_row = lax.broadcasted_iota(jnp.int32, (RG_SEG, RG_CB), 0)
    state = carry_ref[...]
    enter = jnp.zeros((RG_SEG, RG_CB), F32)
    for c in (range(RG_SEG - 1, -1, -1) if rev else range(RG_SEG)):
        enter = jnp.where(seg_row == c, state, enter)
        state = hloc[c:c + 1] + p[c:c + 1] * state
    carry_ref[...] = state

    @pl.when(s >= 1)
    def _():
        for r in range(GRID_H):
            h_ref[r] = b_ref[r] + a_ref[r] * enter


def _rg_scan(xl3, xc3, wa, wx, ba, bx, lam, rev):
    n_segblk = GRID_W // RG_SEG

    def seg_block(s):
        return jnp.clip((n_segblk - s) if rev else (s - 1), 0, n_segblk - 1)

    blk = (GRID_H, RG_SEG, RG_CB)
    nk = RG_CB // RG_BLOCK_P
    wspec = lambda: pl.BlockSpec((nk, RG_BLOCK_P, RG_BLOCK_P), lambda cb, b, s: (cb, 0, 0))
    vspec = lambda: pl.BlockSpec((1, RG_CB), lambda cb, b, s: (0, cb))
    return pl.pallas_call(
        functools.partial(_rg_scan_kernel, rev=rev),
        grid=(RG_WIDTH_P // RG_CB, BATCH, n_segblk + 1),
        in_specs=[pl.BlockSpec(blk, lambda cb, b, s: (b, seg_block(s), cb)),
                  pl.BlockSpec(blk, lambda cb, b, s: (b, 0, cb)),
                  wspec(), wspec(), vspec(), vspec(), vspec()],
        out_specs=pl.BlockSpec(blk, lambda cb, b, s: (b, seg_block(s), cb)),
        out_shape=jax.ShapeDtypeStruct((BATCH * GRID_H, GRID_W, RG_WIDTH_P), F32),
        scratch_shapes=[pltpu.VMEM((1, RG_CB), F32), pltpu.VMEM(blk, F32), pltpu.VMEM(blk, F32)],
        compiler_params=_params(("parallel", "parallel", "arbitrary"), 40 << 20),
        name="rg_scan_bwd" if rev else "rg_scan_fwd",
    )(xl3, xc3, wa, wx, ba, bx, lam)


def _rg_gate_kernel(g_ref, hf_ref, hb_ref, o_ref):
    g = g_ref[...].astype(F32)
    o_ref[...] = (jax.nn.gelu(g) * (hf_ref[...] + hb_ref[...])).astype(BF16)


def _rg_gate(gg, hf, hb, bm=512, bn=1536):
    return pl.pallas_call(
        _rg_gate_kernel,
        grid=(N_LAT // bm, RG_WIDTH_P // bn),
        in_specs=[pl.BlockSpec((bm, bn), lambda i, j: (i, j)),
                  pl.BlockSpec((bm, bn), lambda i, j: (i, j)),
                  pl.BlockSpec((bm, bn), lambda i, j: (i, j))],
        out_specs=pl.BlockSpec((bm, bn), lambda i, j: (i, j)),
        out_shape=jax.ShapeDtypeStruct((N_LAT, RG_WIDTH_P), BF16),
        compiler_params=_params(("parallel", "parallel"), 32 << 20),
        name="rg_gate",
    )(gg, hf, hb)


def _pad_cols_kernel(w_ref, o_ref):
    w = w_ref[...]
    zeros = jnp.zeros((w.shape[0], RG_BLOCK_P - RG_BLOCK), BF16)
    for k in range(RG_N_BLOCKS):
        blk = w[:, k * RG_BLOCK:(k + 1) * RG_BLOCK].astype(BF16)
        o_ref[:, k * RG_BLOCK_P:(k + 1) * RG_BLOCK_P] = jnp.concatenate([blk, zeros], axis=1)


def _pad_cols(w, col0, bm=512):
    rows = w.shape[0]
    return pl.pallas_call(
        _pad_cols_kernel,
        grid=(rows // bm,),
        in_specs=[pl.BlockSpec((pl.Element(bm), pl.Element(RG_WIDTH)), lambda i: (i * bm, col0))],
        out_specs=pl.BlockSpec((bm, RG_WIDTH_P), lambda i: (i, 0)),
        out_shape=jax.ShapeDtypeStruct((rows, RG_WIDTH_P), BF16),
        compiler_params=_params(("parallel",), 48 << 20),
        name="pad_rg_cols",
    )(w)


def _pad_rg(a, axis):
    axis = axis % a.ndim
    shape = a.shape[:axis] + (RG_N_BLOCKS, RG_BLOCK) + a.shape[axis + 1:]
    pad = [(0, 0)] * (a.ndim + 1)
    pad[axis + 1] = (0, RG_BLOCK_P - RG_BLOCK)
    out = jnp.pad(a.reshape(shape), pad)
    return out.reshape(a.shape[:axis] + (RG_WIDTH_P,) + a.shape[axis + 1:])


def _group_lanes(a):
    lead = a.shape[:-1]
    e = SSD_HEADS_PER_GROUP
    t = a.reshape(lead + (2, SSD_N_GROUPS, e))
    t = jnp.moveaxis(t, -3, -2).reshape(lead + (SSD_N_GROUPS, 2 * e))
    t = jnp.pad(t, [(0, 0)] * len(lead) + [(0, 0), (0, 128 - 2 * e)])
    return t.reshape(lead + (DT_W,))


def kernel(x, c, ctx, c_ctx, w_ada, b_ada, norm_g, ffn_w_up, ffn_w_down, w_in, ssd_conv_w, ssd_conv_b, ssd_dt_bias,
           ssd_a_log, ssd_d, ssd_norm_g, w_ssd_out, rg_conv_w, rg_conv_b, rg_w_a, rg_b_a, rg_w_x, rg_b_x, rg_lam,
           w_rg_out, w_out):
    l = 0
    g = norm_g[l]

    wi = w_in[l]
    w_dt = _group_lanes(wi[:, _S2:_S3]).astype(BF16)
    w_gg = _pad_cols(wi, _S3)
    w_xr = _pad_cols(wi, _S4)
    dt_bias_p = _group_lanes(ssd_dt_bias[l]).reshape(1, DT_W)
    alog_p = _group_lanes(ssd_a_log[l].reshape(2 * SSD_N_HEADS)).reshape(1, DT_W)
    d_exp = jnp.repeat(ssd_d[l], SSD_HEAD_DIM).reshape(1, SSD_D_INNER)
    ssd_ng = ssd_norm_g[l].reshape(1, SSD_D_INNER)
    ssd_cw, ssd_cb = ssd_conv_w[l], ssd_conv_b[l].reshape(1, SSD_CONV_DIM)
    rg_cw, rg_cb = _pad_rg(rg_conv_w[l], 1), _pad_rg(rg_conv_b[l], 0).reshape(1, RG_WIDTH_P)
    pad_w = lambda w: jnp.pad(w, ((0, 0), (0, RG_BLOCK_P - RG_BLOCK), (0, RG_BLOCK_P - RG_BLOCK))).astype(BF16)
    rg_wa = [pad_w(rg_w_a[l, d]) for d in range(2)]
    rg_wx = [pad_w(rg_w_x[l, d]) for d in range(2)]
    rg_ba = [_pad_rg(rg_b_a[l, d], 0).reshape(1, RG_WIDTH_P) for d in range(2)]
    rg_bx = [_pad_rg(rg_b_x[l, d], 0).reshape(1, RG_WIDTH_P) for d in range(2)]
    rg_lm = [_pad_rg(rg_lam[l, d], 0).reshape(1, RG_WIDTH_P) for d in range(2)]
    w_rg_o = _pad_rg(w_rg_out[l].astype(BF16), 0)
    w_o = w_out[l].astype(BF16)

    c8 = jnp.concatenate([c, c_ctx[None], jnp.zeros((8 - BATCH - 1, D_MODEL), F32)], axis=0)
    mod3 = _ada(c8, w_ada[l], b_ada[l].reshape(1, -1)).reshape(8, 1, N_MOD * D_MODEL)

    h0 = (x.reshape(N_LAT, D_MODEL), ctx.reshape(N_CTX, D_MODEL))

    u1 = _normmod(h0, g[0], mod3, 0, N_TOK)
    y1 = _ffn(u1, ffn_w_up, ffn_w_down, (l, 0))
    h1, u2 = _residual(h0, y1, g[1], mod3, 0, MACARON_W, nxt=(g[2], 1))

    direct = functools.partial(_matmul, u2, w_in, w_lead=(l,), out_dtype=BF16, single_buffer_x=True)
    z = direct(rows=N_LAT, bm=N_LAT // 4, bn=512, w_col0=0, n=_S1, name="in_proj_z")
    xbc = direct(rows=N_TOK, bm=N_TOK // 4, bn=512, w_col0=_S1, n=SSD_CONV_DIM, name="in_proj_xbc")
    mg, w_ssd_o = direct(rows=N_LAT, bm=N_LAT // 4, bn=256, w_col0=_S5, n=2 * D_MODEL, side=(w_ssd_out, (l,)),
                         name="in_proj_merge_gates")
    proj = functools.partial(_matmul, u2, bm=1024, bn=1024)
    gg = proj(w_gg, rows=N_LAT, out_dtype=BF16, name="in_proj_gelu_gate")
    xr = proj(w_xr, rows=N_TOK, out_dtype=BF16, name="in_proj_rg")
    dt = proj(w_dt, rows=N_TOK, out_dtype=F32, softplus_bias=dt_bias_p, name="in_proj_dt")

    xconv = _conv_rows(xbc, ssd_cw, ssd_cb, row_blk0=0, n_blk=N_TOK // CONV_BM, silu=True, out_dtype=BF16)
    y_f = _ssd_scan(xconv, dt, alog_p, rev=False)
    y_b = _ssd_scan(xconv, dt, alog_p, rev=True)
    y_n = _ssd_gate(y_f, y_b, xconv, z, d_exp, ssd_ng)

    xr_lat = _rg_conv_lat(xr.reshape(N_TOK // GRID_W, GRID_W, RG_WIDTH_P), rg_cw, rg_cb)
    xr_ctx = _conv_rows(xr, rg_cw, rg_cb, row_blk0=N_LAT // CONV_BM, n_blk=N_CTX // CONV_BM, silu=False,
                        out_dtype=F32)
    xr_ctx = xr_ctx.reshape(BATCH, RG_SEG, GRID_H, RG_WIDTH_P).transpose(0, 2, 1, 3)
    xr_ctx = xr_ctx.reshape(BATCH * GRID_H, RG_SEG, RG_WIDTH_P)
    h_f = _rg_scan(xr_lat, xr_ctx, rg_wa[0], rg_wx[0], rg_ba[0], rg_bx[0], rg_lm[0], rev=False)
    h_b = _rg_scan(xr_lat, xr_ctx, rg_wa[1], rg_wx[1], rg_ba[1], rg_bx[1], rg_lm[1], rev=True)
    r_in = _rg_gate(gg, h_f.reshape(N_LAT, RG_WIDTH_P), h_b.reshape(N_LAT, RG_WIDTH_P))

    m1 = _matmul(y_n, w_ssd_o, rows=N_LAT, bm=1024, bn=512, out_dtype=F32, gate=(mg, 0), single_buffer_x=True,
                 name="ssd_out_proj")
    m2 = _matmul(r_in, w_rg_o, rows=N_LAT, bm=1024, bn=512, out_dtype=BF16, gate=(mg, D_MODEL // 512), prev=m1,
                 name="rg_out_proj")
    m3 = _matmul(m2, w_o, rows=N_LAT, bm=1024, bn=512, out_dtype=BF16, name="out_proj")
    h2, u3 = _residual(h1, m3, g[3], mod3, 1, 1.0, nxt=(g[4], 2))

    y3 = _ffn(u3, ffn_w_up, ffn_w_down, (l, 1))
    out = _residual(h2, y3, g[5], mod3, 2, MACARON_W)
    return out.reshape(BATCH, SEQ, D_MODEL)
```

```python
import functools
import math

import jax
import jax.numpy as jnp
from jax import lax
from jax.experimental import pallas as pl
from jax.experimental.pallas import tpu as pltpu

F32 = jnp.float32
BF16 = jnp.bfloat16

D_MODEL = 4096
BATCH = 4
SEQ = 2048
GRID_W = 64
GRID_H = SEQ // GRID_W
CTX_LEN = 256
N_MOD = 9
EPS = 1e-6
MACARON_W = 0.5

N_LAT = BATCH * SEQ
N_CTX = BATCH * CTX_LEN
N_TOK = N_LAT + N_CTX

SSD_D_INNER = 2 * D_MODEL
SSD_HEAD_DIM = 64
SSD_N_HEADS = SSD_D_INNER // SSD_HEAD_DIM
SSD_N_GROUPS = 8
SSD_HEADS_PER_GROUP = SSD_N_HEADS // SSD_N_GROUPS
SSD_GROUP_W = SSD_D_INNER // SSD_N_GROUPS
SSD_D_STATE = 128
SSD_GN = SSD_N_GROUPS * SSD_D_STATE
SSD_CONV_DIM = SSD_D_INNER + 2 * SSD_GN
SSD_CHUNK = 128
SSD_GROUPS_PER_STEP = 8

RG_WIDTH = 5376
RG_N_BLOCKS = 16
RG_BLOCK = RG_WIDTH // RG_N_BLOCKS
RG_BLOCK_P = 384
RG_WIDTH_P = RG_N_BLOCKS * RG_BLOCK_P
RG_C = 8.0
RG_SEG = 8
RG_CB = 4 * RG_BLOCK_P

D_FF = 11008

_S1 = SSD_D_INNER
_S2 = _S1 + SSD_CONV_DIM
_S3 = _S2 + 2 * SSD_N_HEADS
_S4 = _S3 + RG_WIDTH
_S5 = _S4 + RG_WIDTH

DT_W = SSD_N_GROUPS * 128

VMEM_CAP = 56 * 1024 * 1024


def _params(sem, vmem_bytes):
    return pltpu.CompilerParams(dimension_semantics=sem,
                                vmem_limit_bytes=int(min(VMEM_CAP, max(32 << 20, vmem_bytes))))


def _softplus(x):
    return jnp.maximum(x, 0.0) + jnp.log1p(jnp.exp(-jnp.abs(x)))


def _sigmoid(x):
    return 0.5 * jnp.tanh(0.5 * x) + 0.5


def _neg_expm1_2x(x):
    t = jnp.tanh(x)
    return (t + t) / (t - 1.0)


ADA_BN = 512


def _ada_kernel(c_ref, w_ref, b_ref, o_ref):
    c = c_ref[...]
    sc = (c * _sigmoid(c)).astype(BF16)
    o_ref[...] = jnp.dot(sc, w_ref[...].astype(BF16), preferred_element_type=F32) + b_ref[...]


def _ada(c8, w_ada, b_ada):
    n = w_ada.shape[1]
    return pl.pallas_call(
        _ada_kernel,
        grid=(n // ADA_BN,),
        in_specs=[pl.BlockSpec((8, D_MODEL), lambda j: (0, 0)),
                  pl.BlockSpec((D_MODEL, ADA_BN), lambda j: (0, j)),
                  pl.BlockSpec((1, ADA_BN), lambda j: (0, j))],
        out_specs=pl.BlockSpec((8, ADA_BN), lambda j: (0, j)),
        out_shape=jax.ShapeDtypeStruct((8, n), F32),
        compiler_params=_params(("parallel",), 2 * D_MODEL * ADA_BN * 4 + D_MODEL * ADA_BN * 4 + (4 << 20)),
        name="ada_mod",
    )(c8, w_ada, b_ada)


ROW_BM = 256


def _mod_row(i):
    return jnp.where(i < N_LAT // ROW_BM, i // (SEQ // ROW_BM), BATCH)


def _mod_spec(slot):
    return pl.BlockSpec((None, 1, D_MODEL), lambda i: (_mod_row(i), 0, slot))


def _vec_spec():
    return pl.BlockSpec((1, D_MODEL), lambda i: (0, 0))


def _row_spec():
    return pl.BlockSpec((ROW_BM, D_MODEL), lambda i: (i, 0))


N_LAT_BLK = N_LAT // ROW_BM


def _token_specs(h):
    if not isinstance(h, tuple):
        return [_row_spec()], [h]
    return [pl.BlockSpec((ROW_BM, D_MODEL), lambda i: (jnp.minimum(i, N_LAT_BLK - 1), 0)),
            pl.BlockSpec((ROW_BM, D_MODEL), lambda i: (jnp.maximum(i - N_LAT_BLK, 0), 0))], list(h)


def _load_tokens(refs):
    if len(refs) == 1:
        return refs[0][...]
    return jnp.where(pl.program_id(0) < N_LAT_BLK, refs[0][...], refs[1][...])


def _rms(x):
    return x * lax.rsqrt(jnp.mean(x * x, axis=-1, keepdims=True) + EPS)


def _normmod_kernel(*refs, n_h):
    g_ref, sc_ref, sh_ref, u_ref = refs[n_h:]
    u = _rms(_load_tokens(refs[:n_h])) * g_ref[...]
    u_ref[...] = (u * (1.0 + sc_ref[...]) + sh_ref[...]).astype(BF16)


def _normmod(h, g, mod3, slot, rows):
    h_specs, h_args = _token_specs(h)
    return pl.pallas_call(
        functools.partial(_normmod_kernel, n_h=len(h_args)),
        grid=(rows // ROW_BM,),
        in_specs=h_specs + [_vec_spec(), _mod_spec(3 * slot + 1), _mod_spec(3 * slot)],
        out_specs=_row_spec(),
        out_shape=jax.ShapeDtypeStruct((rows, D_MODEL), BF16),
        compiler_params=_params(("parallel",), 40 << 20),
        name="normmod",
    )(*h_args, g.reshape(1, D_MODEL), mod3, mod3)


def _res_kernel(*refs, n_h, coef, with_next):
    y_ref, gpost_ref, gate_ref = refs[n_h:n_h + 3]
    rest = refs[n_h + 3:]
    h2 = _load_tokens(refs[:n_h]) + (coef * gate_ref[...]) * (_rms(y_ref[...].astype(F32)) * gpost_ref[...])
    if with_next:
        gn_ref, sc_ref, sh_ref, ho_ref, u_ref = rest
        ho_ref[...] = h2
        u = _rms(h2) * gn_ref[...]
        u_ref[...] = (u * (1.0 + sc_ref[...]) + sh_ref[...]).astype(BF16)
    else:
        (ho_ref,) = rest
        ho_ref[...] = h2


def _residual(h, y, g_post, mod3, slot, coef, nxt=None):
    rows = y.shape[0]
    h_specs, h_args = _token_specs(h)
    in_specs = h_specs + [_row_spec(), _vec_spec(), _mod_spec(3 * slot + 2)]
    args = h_args + [y, g_post.reshape(1, D_MODEL), mod3]
    out_specs = [_row_spec()]
    out_shape = [jax.ShapeDtypeStruct((rows, D_MODEL), F32)]
    if nxt is not None:
        g_next, slot_next = nxt
        in_specs += [_vec_spec(), _mod_spec(3 * slot_next + 1), _mod_spec(3 * slot_next)]
        args += [g_next.reshape(1, D_MODEL), mod3, mod3]
        out_specs.append(_row_spec())
        out_shape.append(jax.ShapeDtypeStruct((rows, D_MODEL), BF16))
    out = pl.pallas_call(
        functools.partial(_res_kernel, n_h=len(h_args), coef=coef, with_next=nxt is not None),
        grid=(rows // ROW_BM,),
        in_specs=in_specs, out_specs=out_specs, out_shape=out_shape,
        compiler_params=_params(("parallel",), 48 << 20),
        name="residual_norm",
    )(*args)
    return out if nxt is not None else out[0]


def _w_spec(lead, k, bn, col_blk0):
    squeezed = (None,) * len(lead)
    return pl.BlockSpec(squeezed + (k, bn), lambda i, j: tuple(lead) + (0, col_blk0 + j))


def _side_cast_specs(side, n_col_blocks):
    arr, lead = side
    rows, cols = arr.shape[-2:]
    r = rows // n_col_blocks

    def blk(i, j):
        return jnp.where(i == 0, j, n_col_blocks - 1)

    squeezed = (None,) * len(lead)
    in_spec = pl.BlockSpec(squeezed + (r, cols), lambda i, j: tuple(lead) + (blk(i, j), 0))
    out_spec = pl.BlockSpec((r, cols), lambda i, j: (blk(i, j), 0))
    return in_spec, out_spec, jax.ShapeDtypeStruct((rows, cols), BF16), 2 * r * cols * (4 + 2)


def _side_cast(side_ref, side_o_ref):
    @pl.when(pl.program_id(0) == 0)
    def _():
        side_o_ref[...] = side_ref[...].astype(BF16)


def _up_kernel(x_ref, wg_ref, wu_ref, side_ref, o_ref, side_o_ref):
    x = x_ref[...]
    g = jnp.dot(x, wg_ref[...].astype(BF16), preferred_element_type=F32)
    u = jnp.dot(x, wu_ref[...].astype(BF16), preferred_element_type=F32)
    o_ref[...] = (g * _sigmoid(g) * u).astype(BF16)
    _side_cast(side_ref, side_o_ref)


FFN_UP_BN = 256


def _ffn_up(u, w_up, w_down, lead):
    rows, k = u.shape
    bm, bn = rows // 4, FFN_UP_BN
    side_in, side_out, side_shape, side_bytes = _side_cast_specs((w_down, lead), D_FF // bn)
    vmem = (bm * k * 2 + 4 * k * bn * 4 + 2 * k * bn * 2 + 2 * bm * bn * 2 + 3 * bm * bn * 4 + side_bytes
            + (4 << 20))
    return pl.pallas_call(
        _up_kernel,
        grid=(rows // bm, D_FF // bn),
        in_specs=[pl.BlockSpec((bm, k), lambda i, j: (i, 0), pipeline_mode=pl.Buffered(1)),
                  _w_spec(lead, k, bn, 0),
                  _w_spec(lead, k, bn, D_FF // bn),
                  side_in],
        out_specs=[pl.BlockSpec((bm, bn), lambda i, j: (i, j)), side_out],
        out_shape=[jax.ShapeDtypeStruct((rows, D_FF), BF16), side_shape],
        compiler_params=_params(("arbitrary", "arbitrary"), vmem),
        name="ffn_up_swiglu",
    )(u, w_up, w_up, w_down)


def _mm_kernel(*refs, mode, has_side):
    refs = list(refs)
    if has_side:
        side_o_ref = refs.pop()
    o_ref = refs.pop()
    if has_side:
        _side_cast(refs.pop(), side_o_ref)
    x_ref, w_ref, *rest = refs
    acc = jnp.dot(x_ref[...], w_ref[...].astype(BF16), preferred_element_type=F32)
    if mode == "softplus_bias":
        (b_ref,) = rest
        acc = _softplus(acc + b_ref[...])
    elif mode == "gate":
        (g_ref,) = rest
        acc = _sigmoid(g_ref[...].astype(F32)) * acc
    elif mode == "gate_add":
        g_ref, p_ref = rest
        acc = p_ref[...] + _sigmoid(g_ref[...].astype(F32)) * acc
    o_ref[...] = acc.astype(o_ref.dtype)


def _matmul(x, w, *, rows, bm, bn, out_dtype, gate=None, prev=None, softplus_bias=None, single_buffer_x=False,
            w_lead=(), w_col0=0, n=None, side=None, name="matmul"):
    k = x.shape[1]
    n = w.shape[-1] if n is None else n
    mode = "plain" if gate is None else ("gate" if prev is None else "gate_add")
    x_kwargs = dict(pipeline_mode=pl.Buffered(1)) if single_buffer_x else {}
    in_specs = [pl.BlockSpec((bm, k), lambda i, j: (i, 0), **x_kwargs),
                _w_spec(w_lead, k, bn, w_col0 // bn)]
    args = [x, w]
    if softplus_bias is not None:
        mode = "softplus_bias"
        in_specs.append(pl.BlockSpec((1, bn), lambda i, j: (0, j)))
        args.append(softplus_bias)
    if gate is not None:
        g_arr, g_blk0 = gate
        in_specs.append(pl.BlockSpec((bm, bn), lambda i, j: (i, g_blk0 + j)))
        args.append(g_arr)
    if prev is not None:
        in_specs.append(pl.BlockSpec((bm, bn), lambda i, j: (i, j)))
        args.append(prev)
    out_specs = [pl.BlockSpec((bm, bn), lambda i, j: (i, j))]
    out_shape = [jax.ShapeDtypeStruct((rows, n), out_dtype)]
    xbuf = 1 if single_buffer_x else 2
    w_bytes = 2 * k * bn * w.dtype.itemsize + (k * bn * 2 if w.dtype != BF16 else 0)
    vmem = xbuf * bm * k * 2 + w_bytes + 5 * bm * bn * 4 + (4 << 20)
    if side is not None:
        side_in, side_out, side_shape, side_bytes = _side_cast_specs(side, n // bn)
        in_specs.append(side_in)
        args.append(side[0])
        out_specs.append(side_out)
        out_shape.append(side_shape)
        vmem += side_bytes
    out = pl.pallas_call(
        functools.partial(_mm_kernel, mode=mode, has_side=side is not None),
        grid=(rows // bm, n // bn),
        in_specs=in_specs,
        out_specs=out_specs,
        out_shape=out_shape,
        compiler_params=_params(("arbitrary", "arbitrary") if side is not None else ("parallel", "parallel"), vmem),
        name=name,
    )(*args)
    return out if side is not None else out[0]


def _ffn(u, w_up, w_down, lead):
    act, wd = _ffn_up(u, w_up, w_down, lead)
    return _matmul(act, wd, rows=act.shape[0], bm=1024, bn=512, out_dtype=BF16, single_buffer_x=True,
                   name="ffn_down")


CONV_BM = CTX_LEN
CONV_HALO = 16
CONV_CB = 2048


def _conv_rows_kernel(x_ref, prev_ref, next_ref, w_ref, b_ref, o_ref, *, row_blk0, silu):
    i = row_blk0 + pl.program_id(0)
    lat_blocks = N_LAT // CONV_BM
    per_seq = SEQ // CONV_BM
    is_ctx = i >= lat_blocks
    starts = jnp.logical_or(is_ctx, i % per_seq == 0)
    ends = jnp.logical_or(is_ctx, i % per_seq == per_seq - 1)
    x = x_ref[...].astype(F32)
    before = jnp.where(starts, 0.0, prev_ref[...].astype(F32)[CONV_HALO - 8:])
    after = jnp.where(ends, 0.0, next_ref[...].astype(F32)[:8])
    t = x.shape[0]
    w = w_ref[...]
    bias = b_ref[...]

    def taps(v):
        acc = v * w[2:3] + bias
        acc += pltpu.roll(v, 2, axis=0) * w[0:1]
        acc += pltpu.roll(v, 1, axis=0) * w[1:2]
        acc += pltpu.roll(v, v.shape[0] - 1, axis=0) * w[3:4]
        return acc * _sigmoid(acc) if silu else acc

    o_ref[...] = taps(jnp.concatenate([before, x, after], axis=0))[8:8 + t].astype(o_ref.dtype)


def _conv_rows(src, w, b, *, row_blk0, n_blk, silu, out_dtype):
    width = src.shape[1]
    halo_per_blk = CONV_BM // CONV_HALO
    last_halo = src.shape[0] // CONV_HALO - 1
    return pl.pallas_call(
        functools.partial(_conv_rows_kernel, row_blk0=row_blk0, silu=silu),
        grid=(n_blk, width // CONV_CB),
        in_specs=[pl.BlockSpec((CONV_BM, CONV_CB), lambda i, j: (row_blk0 + i, j)),
                  pl.BlockSpec((CONV_HALO, CONV_CB),
                               lambda i, j: (jnp.maximum((row_blk0 + i) * halo_per_blk - 1, 0), j)),
                  pl.BlockSpec((CONV_HALO, CONV_CB),
                               lambda i, j: (jnp.minimum((row_blk0 + i + 1) * halo_per_blk, last_halo), j)),
                  pl.BlockSpec((4, CONV_CB), lambda i, j: (0, j)),
                  pl.BlockSpec((1, CONV_CB), lambda i, j: (0, j))],
        out_specs=pl.BlockSpec((CONV_BM, CONV_CB), lambda i, j: (i, j)),
        out_shape=jax.ShapeDtypeStruct((n_blk * CONV_BM, width), out_dtype),
        compiler_params=_params(("parallel", "parallel"), 32 << 20),
        name="conv_rows",
    )(src, src, src, w, b)


def _split3(v):
    hi = v.astype(BF16)
    r1 = v - hi.astype(F32)
    mid = r1.astype(BF16)
    lo = (r1 - mid.astype(F32)).astype(BF16)
    return hi, mid, lo


def _dot_exact_rhs(lhs_bf16, v, terms=3):
    out = None
    for piece in _split3(v)[:terms]:
        d = jnp.dot(lhs_bf16, piece, preferred_element_type=F32)
        out = d if out is None else out + d
    return out


def _ssd_kernel(x_ref, b_ref, c_ref, dt_ref, alog_ref, y_ref, state_ref, *, rev):
    L = SSD_CHUNK
    E = SSD_HEADS_PER_GROUP
    P = SSD_HEAD_DIM
    GW = SSD_GROUP_W
    off = E if rev else 0
    s = pl.program_id(2)

    @pl.when(s == 0)
    def _():
        state_ref[...] = jnp.zeros_like(state_ref)

    ii = lax.broadcasted_iota(jnp.int32, (L, L), 0)
    jj = lax.broadcasted_iota(jnp.int32, (L, L), 1)
    tri = (jj >= ii) if rev else (jj <= ii)
    tri_b = jnp.where(tri, 1.0, 0.0).astype(BF16)
    first_half = lax.broadcasted_iota(jnp.int32, (L, 2 * P), 1) < P

    groups = []
    for gi in range(SSD_GROUPS_PER_STEP):
        tile = slice(gi * 128, (gi + 1) * 128)
        dt = dt_ref[:, tile]
        da = dt * (-jnp.exp(alog_ref[:, tile]))
        cs = _dot_exact_rhs(tri_b, da)
        tot = cs[0:1] if rev else cs[L - 1:L]
        log_end = jnp.log(dt) + (tot - cs)
        groups.append((dt, cs, tot, log_end, b_ref[:, tile], c_ref[:, tile]))

    @pl.when(s >= 2)
    def _():
        for gi, (dt, cs, tot, log_end, bm, cm) in enumerate(groups):
            cs_t = cs.T
            dt_t = dt.T
            cb = lax.dot_general(cm, bm, (((1,), (1,)), ((), ())), preferred_element_type=F32)
            for pair in range(E // 2):
                sl = slice(gi * GW + pair * 2 * P, gi * GW + (pair + 1) * 2 * P)
                xp = x_ref[:, sl]
                halves, cols = [], []
                for e in (off + 2 * pair, off + 2 * pair + 1):
                    col = jnp.broadcast_to(cs[:, e:e + 1], (L, L))
                    cols.append(col)
                    lmat = jnp.exp(jnp.where(tri, col - cs_t[e:e + 1, :], -1e30))
                    m = (cb * lmat * dt_t[e:e + 1, :]).astype(BF16)
                    halves.append(jnp.dot(m, xp, preferred_element_type=F32))
                from_start = jnp.exp(jnp.where(first_half, cols[0], cols[1]))
                y_off = jnp.dot(cm, state_ref[:, sl].astype(BF16), preferred_element_type=F32) * from_start
                y_ref[:, sl] = (jnp.where(first_half, halves[0], halves[1]) + y_off).astype(y_ref.dtype)

    for gi, (dt, cs, tot, log_end, bm, cm) in enumerate(groups):
        for pair in range(E // 2):
            sl = slice(gi * GW + pair * 2 * P, gi * GW + (pair + 1) * 2 * P)
            e0 = off + 2 * pair
            w_end = jnp.exp(jnp.where(first_half, jnp.broadcast_to(log_end[:, e0:e0 + 1], (L, 2 * P)),
                                      jnp.broadcast_to(log_end[:, e0 + 1:e0 + 2], (L, 2 * P))))
            x_end = (x_ref[:, sl].astype(F32) * w_end).astype(BF16)
            upd = lax.dot_general(bm, x_end, (((0,), (0,)), ((), ())), preferred_element_type=F32)
            tot_pair = jnp.where(first_half[0:1], jnp.broadcast_to(tot[:, e0:e0 + 1], (1, 2 * P)),
                                 jnp.broadcast_to(tot[:, e0 + 1:e0 + 2], (1, 2 * P)))
            state_ref[:, sl] = state_ref[:, sl] * jnp.exp(tot_pair) + upd


def _ssd_row_block(b, s, rev):
    n_lat = SEQ // SSD_CHUNK
    if rev:
        ctx = N_LAT // SSD_CHUNK + 2 * b + (1 - s)
        lat = n_lat * b + (n_lat + 1 - s)
    else:
        ctx = N_LAT // SSD_CHUNK + 2 * b + s
        lat = n_lat * b + (s - 2)
    return jnp.where(s < 2, ctx, lat)


def _ssd_scan(xconv, dt, alog_p, rev):
    n_lat = SEQ // SSD_CHUNK
    gps = SSD_GROUPS_PER_STEP
    rb = functools.partial(_ssd_row_block, rev=rev)

    def out_block(b, s):
        return _ssd_row_block(b, jnp.maximum(s, 2), rev)

    bw = gps * SSD_D_STATE
    b0 = SSD_D_INNER // bw
    c0 = (SSD_D_INNER + SSD_GN) // bw
    return pl.pallas_call(
        functools.partial(_ssd_kernel, rev=rev),
        grid=(BATCH, SSD_N_GROUPS // gps, n_lat + 2),
        in_specs=[pl.BlockSpec((SSD_CHUNK, gps * SSD_GROUP_W), lambda b, g, s: (rb(b, s), g)),
                  pl.BlockSpec((SSD_CHUNK, bw), lambda b, g, s: (rb(b, s), b0 + g)),
                  pl.BlockSpec((SSD_CHUNK, bw), lambda b, g, s: (rb(b, s), c0 + g)),
                  pl.BlockSpec((SSD_CHUNK, gps * 128), lambda b, g, s: (rb(b, s), g)),
                  pl.BlockSpec((1, gps * 128), lambda b, g, s: (0, g))],
        out_specs=pl.BlockSpec((SSD_CHUNK, gps * SSD_GROUP_W), lambda b, g, s: (out_block(b, s), g)),
        out_shape=jax.ShapeDtypeStruct((N_LAT, SSD_D_INNER), BF16),
        scratch_shapes=[pltpu.VMEM((SSD_D_STATE, gps * SSD_GROUP_W), F32)],
        compiler_params=_params(("parallel", "parallel", "arbitrary"), 32 << 20),
        name="ssd_scan_bwd" if rev else "ssd_scan_fwd",
    )(xconv, xconv, xconv, dt, alog_p)


def _ssd_gate_kernel(yf_ref, yb_ref, xs_ref, z_ref, d_ref, g_ref, o_ref):
    z = z_ref[...].astype(F32)
    y = yf_ref[...].astype(F32) + yb_ref[...].astype(F32) + d_ref[...] * xs_ref[...].astype(F32)
    y = y * (z * _sigmoid(z))
    o_ref[...] = (_rms(y) * g_ref[...]).astype(BF16)


def _ssd_gate(yf, yb, xconv, proj, d_exp, norm_g, bm=512):
    blk = lambda: pl.BlockSpec((bm, SSD_GROUP_W), lambda i, g: (i, g))
    vec = lambda: pl.BlockSpec((1, SSD_GROUP_W), lambda i, g: (0, g))
    return pl.pallas_call(
        _ssd_gate_kernel,
        grid=(N_LAT // bm, SSD_N_GROUPS),
        in_specs=[blk(), blk(), blk(), blk(), vec(), vec()],
        out_specs=blk(),
        out_shape=jax.ShapeDtypeStruct((N_LAT, SSD_D_INNER), BF16),
        compiler_params=_params(("parallel", "parallel"), 32 << 20),
        name="ssd_gate_norm",
    )(yf, yb, xconv, proj, d_exp, norm_g)


def _rg_conv_lat_kernel(x_ref, w_ref, b_ref, o_ref):
    w = w_ref[...]
    bias = b_ref[...]
    col = lax.broadcasted_iota(jnp.int32, (GRID_W, x_ref.shape[2]), 0)

    def row(r):
        return x_ref[r].astype(F32)

    def prev_col(v):
        return jnp.where(col >= 1, pltpu.roll(v, 1, axis=0), 0.0)

    def next_col(v):
        return jnp.where(col < GRID_W - 1, pltpu.roll(v, GRID_W - 1, axis=0), 0.0)

    for r in range(GRID_H):
        m2 = row(r - 2) if r >= 2 else prev_col(row(r - 2 + GRID_H))
        m1 = row(r - 1) if r >= 1 else prev_col(row(GRID_H - 1))
        p1 = row(r + 1) if r < GRID_H - 1 else next_col(row(0))
        o_ref[r] = m2 * w[0:1] + m1 * w[1:2] + row(r) * w[2:3] + p1 * w[3:4] + bias


def _rg_conv_lat(xr3, w, b, cb=512):
    return pl.pallas_call(
        _rg_conv_lat_kernel,
        grid=(BATCH, RG_WIDTH_P // cb),
        in_specs=[pl.BlockSpec((GRID_H, GRID_W, cb), lambda bb, j: (bb, 0, j)),
                  pl.BlockSpec((4, cb), lambda bb, j: (0, j)),
                  pl.BlockSpec((1, cb), lambda bb, j: (0, j))],
        out_specs=pl.BlockSpec((GRID_H, GRID_W, cb), lambda bb, j: (bb, 0, j)),
        out_shape=jax.ShapeDtypeStruct((BATCH * GRID_H, GRID_W, RG_WIDTH_P), F32),
        compiler_params=_params(("parallel", "parallel"), 32 << 20),
        name="rg_conv_lat",
    )(xr3, w, b)


def _rg_scan_kernel(xl_ref, xc_ref, wa_ref, wx_ref, ba_ref, bx_ref, lam_ref, h_ref, carry_ref, a_ref, b_ref, *, rev):
    s = pl.program_id(2)

    @pl.when(s == 0)
    def _():
        carry_ref[...] = jnp.zeros_like(carry_ref)

    x3 = jnp.where(s == 0, xc_ref[...], xl_ref[...])
    x2 = x3.reshape(GRID_H * RG_SEG, RG_CB)
    xb = x2.astype(BF16)
    neg_c_sp = -RG_C * _softplus(-lam_ref[...])
    for k in range(RG_CB // RG_BLOCK_P):
        sl = slice(k * RG_BLOCK_P, (k + 1) * RG_BLOCK_P)
        xk = xb[:, sl]
        r = _sigmoid(jnp.dot(xk, wa_ref[k], preferred_element_type=F32) + ba_ref[:, sl])
        i = _sigmoid(jnp.dot(xk, wx_ref[k], preferred_element_type=F32) + bx_ref[:, sl])
        log_a = neg_c_sp[:, sl] * r
        a_ref[:, :, sl] = jnp.exp(log_a).reshape(GRID_H, RG_SEG, RG_BLOCK_P)
        q = _neg_expm1_2x(log_a)
        root = jnp.where(q > 0.0, q * lax.rsqrt(q), 0.0)
        b_ref[:, :, sl] = (root * (i * x2[:, sl])).reshape(GRID_H, RG_SEG, RG_BLOCK_P)

    order = range(GRID_H - 1, -1, -1) if rev else range(GRID_H)
    p = None
    for r in order:
        a = a_ref[r]
        if p is None:
            p, hloc = a, b_ref[r]
        else:
            p, hloc = a * p, a * hloc + b_ref[r]
        a_ref[r] = p
        b_ref[r] = hloc
    seg_row = lax.broadcasted_iota(jnp.int32, (RG_SEG, RG_CB), 0)
    state = carry_ref[...]
    enter = jnp.zeros((RG_SEG, RG_CB), F32)
    for c in (range(RG_SEG - 1, -1, -1) if rev else range(RG_SEG)):
        enter = jnp.where(seg_row == c, state, enter)
        state = hloc[c:c + 1] + p[c:c + 1] * state
    carry_ref[...] = state

    @pl.when(s >= 1)
    def _():
        for r in range(GRID_H):
            h_ref[r] = b_ref[r] + a_ref[r] * enter


def _rg_scan(xl3, xc3, wa, wx, ba, bx, lam, rev):
    n_segblk = GRID_W // RG_SEG

    def seg_block(s):
        return jnp.clip((n_segblk - s) if rev else (s - 1), 0, n_segblk - 1)

    blk = (GRID_H, RG_SEG, RG_CB)
    nk = RG_CB // RG_BLOCK_P
    wspec = lambda: pl.BlockSpec((nk, RG_BLOCK_P, RG_BLOCK_P), lambda cb, b, s: (cb, 0, 0))
    vspec = lambda: pl.BlockSpec((1, RG_CB), lambda cb, b, s: (0, cb))
    return pl.pallas_call(
        functools.partial(_rg_scan_kernel, rev=rev),
        grid=(RG_WIDTH_P // RG_CB, BATCH, n_segblk + 1),
        in_specs=[pl.BlockSpec(blk, lambda cb, b, s: (b, seg_block(s), cb)),
                  pl.BlockSpec(blk, lambda cb, b, s: (b, 0, cb)),
                  wspec(), wspec(), vspec(), vspec(), vspec()],
        out_specs=pl.BlockSpec(blk, lambda cb, b, s: (b, seg_block(s), cb)),
        out_shape=jax.ShapeDtypeStruct((BATCH * GRID_H, GRID_W, RG_WIDTH_P), F32),
        scratch_shapes=[pltpu.VMEM((1, RG_CB), F32), pltpu.VMEM(blk, F32), pltpu.VMEM(blk, F32)],
        compiler_params=_params(("parallel", "parallel", "arbitrary"), 40 << 20),
        name="rg_scan_bwd" if rev else "rg_scan_fwd",
    )(xl3, xc3, wa, wx, ba, bx, lam)


def _rg_gate_kernel(g_ref, hf_ref, hb_ref, o_ref):
    g = g_ref[...].astype(F32)
    o_ref[...] = (jax.nn.gelu(g) * (hf_ref[...] + hb_ref[...])).astype(BF16)


def _rg_gate(gg, hf, hb, bm=512, bn=1536):
    return pl.pallas_call(
        _rg_gate_kernel,
        grid=(N_LAT // bm, RG_WIDTH_P // bn),
        in_specs=[pl.BlockSpec((bm, bn), lambda i, j: (i, j)),
                  pl.BlockSpec((bm, bn), lambda i, j: (i, j)),
                  pl.BlockSpec((bm, bn), lambda i, j: (i, j))],
        out_specs=pl.BlockSpec((bm, bn), lambda i, j: (i, j)),
        out_shape=jax.ShapeDtypeStruct((N_LAT, RG_WIDTH_P), BF16),
        compiler_params=_params(("parallel", "parallel"), 32 << 20),
        name="rg_gate",
    )(gg, hf, hb)


def _pad_cols_kernel(w_ref, o_ref):
    w = w_ref[...]
    zeros = jnp.zeros((w.shape[0], RG_BLOCK_P - RG_BLOCK), BF16)
    for k in range(RG_N_BLOCKS):
        blk = w[:, k * RG_BLOCK:(k + 1) * RG_BLOCK].astype(BF16)
        o_ref[:, k * RG_BLOCK_P:(k + 1) * RG_BLOCK_P] = jnp.concatenate([blk, zeros], axis=1)


def _pad_cols(w, col0, bm=512):
    rows = w.shape[0]
    return pl.pallas_call(
        _pad_cols_kernel,
        grid=(rows // bm,),
        in_specs=[pl.BlockSpec((pl.Element(bm), pl.Element(RG_WIDTH)), lambda i: (i * bm, col0))],
        out_specs=pl.BlockSpec((bm, RG_WIDTH_P), lambda i: (i, 0)),
        out_shape=jax.ShapeDtypeStruct((rows, RG_WIDTH_P), BF16),
        compiler_params=_params(("parallel",), 48 << 20),
        name="pad_rg_cols",
    )(w)


def _pad_rg(a, axis):
    axis = axis % a.ndim
    shape = a.shape[:axis] + (RG_N_BLOCKS, RG_BLOCK) + a.shape[axis + 1:]
    pad = [(0, 0)] * (a.ndim + 1)
    pad[axis + 1] = (0, RG_BLOCK_P - RG_BLOCK)
    out = jnp.pad(a.reshape(shape), pad)
    return out.reshape(a.shape[:axis] + (RG_WIDTH_P,) + a.shape[axis + 1:])


def _group_lanes(a):
    lead = a.shape[:-1]
    e = SSD_HEADS_PER_GROUP
    t = a.reshape(lead + (2, SSD_N_GROUPS, e))
    t = jnp.moveaxis(t, -3, -2).reshape(lead + (SSD_N_GROUPS, 2 * e))
    t = jnp.pad(t, [(0, 0)] * len(lead) + [(0, 0), (0, 128 - 2 * e)])
    return t.reshape(lead + (DT_W,))


def kernel(x, c, ctx, c_ctx, w_ada, b_ada, norm_g, ffn_w_up, ffn_w_down, w_in, ssd_conv_w, ssd_conv_b, ssd_dt_bias,
           ssd_a_log, ssd_d, ssd_norm_g, w_ssd_out, rg_conv_w, rg_conv_b, rg_w_a, rg_b_a, rg_w_x, rg_b_x, rg_lam,
           w_rg_out, w_out):
    l = 0
    g = norm_g[l]

    wi = w_in[l]
    w_dt = _group_lanes(wi[:, _S2:_S3]).astype(BF16)
    w_gg = _pad_cols(wi, _S3)
    w_xr = _pad_cols(wi, _S4)
    dt_bias_p = _group_lanes(ssd_dt_bias[l]).reshape(1, DT_W)
    alog_p = _group_lanes(ssd_a_log[l].reshape(2 * SSD_N_HEADS)).reshape(1, DT_W)
    d_exp = jnp.repeat(ssd_d[l], SSD_HEAD_DIM).reshape(1, SSD_D_INNER)
    ssd_ng = ssd_norm_g[l].reshape(1, SSD_D_INNER)
    ssd_cw, ssd_cb = ssd_conv_w[l], ssd_conv_b[l].reshape(1, SSD_CONV_DIM)
    rg_cw, rg_cb = _pad_rg(rg_conv_w[l], 1), _pad_rg(rg_conv_b[l], 0).reshape(1, RG_WIDTH_P)
    pad_w = lambda w: jnp.pad(w, ((0, 0), (0, RG_BLOCK_P - RG_BLOCK), (0, RG_BLOCK_P - RG_BLOCK))).astype(BF16)
    rg_wa = [pad_w(rg_w_a[l, d]) for d in range(2)]
    rg_wx = [pad_w(rg_w_x[l, d]) for d in range(2)]
    rg_ba = [_pad_rg(rg_b_a[l, d], 0).reshape(1, RG_WIDTH_P) for d in range(2)]
    rg_bx = [_pad_rg(rg_b_x[l, d], 0).reshape(1, RG_WIDTH_P) for d in range(2)]
    rg_lm = [_pad_rg(rg_lam[l, d], 0).reshape(1, RG_WIDTH_P) for d in range(2)]
    w_rg_o = _pad_rg(w_rg_out[l].astype(BF16), 0)
    w_o = w_out[l].astype(BF16)

    c8 = jnp.concatenate([c, c_ctx[None], jnp.zeros((8 - BATCH - 1, D_MODEL), F32)], axis=0)
    mod3 = _ada(c8, w_ada[l], b_ada[l].reshape(1, -1)).reshape(8, 1, N_MOD * D_MODEL)

    h0 = (x.reshape(N_LAT, D_MODEL), ctx.reshape(N_CTX, D_MODEL))

    u1 = _normmod(h0, g[0], mod3, 0, N_TOK)
    y1 = _ffn(u1, ffn_w_up, ffn_w_down, (l, 0))
    h1, u2 = _residual(h0, y1, g[1], mod3, 0, MACARON_W, nxt=(g[2], 1))

    direct = functools.partial(_matmul, u2, w_in, w_lead=(l,), out_dtype=BF16, single_buffer_x=True)
    z = direct(rows=N_LAT, bm=N_LAT // 4, bn=512, w_col0=0, n=_S1, name="in_proj_z")
    xbc = direct(rows=N_TOK, bm=N_TOK // 4, bn=512, w_col0=_S1, n=SSD_CONV_DIM, name="in_proj_xbc")
    mg, w_ssd_o = direct(rows=N_LAT, bm=N_LAT // 4, bn=256, w_col0=_S5, n=2 * D_MODEL, side=(w_ssd_out, (l,)),
                         name="in_proj_merge_gates")
    proj = functools.partial(_matmul, u2, bm=1024, bn=1024)
    gg = proj(w_gg, rows=N_LAT, out_dtype=BF16, name="in_proj_gelu_gate")
    xr = proj(w_xr, rows=N_TOK, out_dtype=BF16, name="in_proj_rg")
    dt = proj(w_dt, rows=N_TOK, out_dtype=F32, softplus_bias=dt_bias_p, name="in_proj_dt")

    xconv = _conv_rows(xbc, ssd_cw, ssd_cb, row_blk0=0, n_blk=N_TOK // CONV_BM, silu=True, out_dtype=BF16)
    y_f = _ssd_scan(xconv, dt, alog_p, rev=False)
    y_b = _ssd_scan(xconv, dt, alog_p, rev=True)
    y_n = _ssd_gate(y_f, y_b, xconv, z, d_exp, ssd_ng)

    xr_lat = _rg_conv_lat(xr.reshape(N_TOK // GRID_W, GRID_W, RG_WIDTH_P), rg_cw, rg_cb)
    xr_ctx = _conv_rows(xr, rg_cw, rg_cb, row_blk0=N_LAT // CONV_BM, n_blk=N_CTX // CONV_BM, silu=False,
                        out_dtype=F32)
    xr_ctx = xr_ctx.reshape(BATCH, RG_SEG, GRID_H, RG_WIDTH_P).transpose(0, 2, 1, 3)
    xr_ctx = xr_ctx.reshape(BATCH * GRID_H, RG_SEG, RG_WIDTH_P)
    h_f = _rg_scan(xr_lat, xr_ctx, rg_wa[0], rg_wx[0], rg_ba[0], rg_bx[0], rg_lm[0], rev=False)
    h_b = _rg_scan(xr_lat, xr_ctx, rg_wa[1], rg_wx[1], rg_ba[1], rg_bx[1], rg_lm[1], rev=True)
    r_in = _rg_gate(gg, h_f.reshape(N_LAT, RG_WIDTH_P), h_b.reshape(N_LAT, RG_WIDTH_P))

    m1 = _matmul(y_n, w_ssd_o, rows=N_LAT, bm=1024, bn=512, out_dtype=F32, gate=(mg, 0), single_buffer_x=True,
                 name="ssd_out_proj")
    m2 = _matmul(r_in, w_rg_o, rows=N_LAT, bm=1024, bn=512, out_dtype=BF16, gate=(mg, D_MODEL // 512), prev=m1,
                 name="rg_out_proj")
    m3 = _matmul(m2, w_o, rows=N_LAT, bm=1024, bn=512, out_dtype=BF16, name="out_proj")
    h2, u3 = _residual(h1, m3, g[3], mod3, 1, 1.0, nxt=(g[4], 2))

    y3 = _ffn(u3, ffn_w_up, ffn_w_down, (l, 1))
    out = _residual(h2, y3, g[5], mod3, 2, MACARON_W)
    return out.reshape(BATCH, SEQ, D_MODEL)
```
